```python
import math
import jax
import jax.numpy as jnp
from jax import lax
import numpy as np


D_MODEL = 2048
BATCH = 1
SEQ = 16384
DEPTH = 1

GRID_W = 64
CTX_LEN = 256
EPS = 1e-6

GMLP_GROUPS = 8
GMLP_CH = D_MODEL // 2 // GMLP_GROUPS
D_A = GMLP_GROUPS * GMLP_CH
CHUNK = 128

DIFF_HEADS = 8
DIFF_QK = D_MODEL // 4 // DIFF_HEADS
DIFF_V = 2 * DIFF_QK
D_QK = DIFF_HEADS * 2 * DIFF_QK
D_V = DIFF_HEADS * DIFF_V
ROPE_BASE = 10000.0
Q_BLOCK = 128

D_MIX = D_A + D_V
D_IN = 2 * D_A + 2 * D_QK + D_V

PEER_HEADS = 8
N_KEYS = 128
N_EXPERTS = N_KEYS * N_KEYS
PEER_DK = 256
PEER_TOPK = 16
PEER_BLOCK = 128

kernel_name = 'hybrid_gmlp_diffattn_peer_dit_block'


def rms_norm(x, g):
    xf = x.astype(jnp.float32)
    y = xf * lax.rsqrt(jnp.mean(xf * xf, axis=-1, keepdims=True) + EPS)
    return (y * g.astype(jnp.float32)).astype(x.dtype)


def group_layer_norm(v, g):
    B, L, _ = v.shape
    vf = v.astype(jnp.float32).reshape(B, L, GMLP_GROUPS, GMLP_CH)
    mu = jnp.mean(vf, axis=-1, keepdims=True)
    d = vf - mu
    y = d * lax.rsqrt(jnp.mean(d * d, axis=-1, keepdims=True) + EPS)
    return (y.reshape(B, L, D_A) * g.astype(jnp.float32)).astype(v.dtype)


def modulate(h, shift, scale):
    return h * (1 + scale) + shift


def adaln(cond, w, b):
    mod = jax.nn.silu(cond) @ w + b
    return [m[:, None, :] for m in jnp.split(mod, 6, axis=-1)]


def axial_rope(t):
    n_tok = t.shape[1]
    n_rows = n_tok // GRID_W
    rows = jnp.repeat(jnp.arange(n_rows, dtype=jnp.float32), GRID_W)
    cols = jnp.tile(jnp.arange(GRID_W, dtype=jnp.float32), n_rows)
    n_freq = DIFF_QK // 4
    inv = ROPE_BASE ** (-jnp.arange(n_freq, dtype=jnp.float32) / n_freq)
    ang = jnp.concatenate([rows[:, None] * inv, cols[:, None] * inv], axis=-1)
    cos = jnp.cos(ang)[None, :, None, None, :]
    sin = jnp.sin(ang)[None, :, None, None, :]
    te = t[..., 0::2].astype(jnp.float32)
    to = t[..., 1::2].astype(jnp.float32)
    out = jnp.stack([te * cos - to * sin, te * sin + to * cos], axis=-1)
    return out.reshape(t.shape).astype(t.dtype)


def qk_heads(z, g):
    B, L, _ = z.shape
    return rms_norm(z.reshape(B, L, DIFF_HEADS, 2, DIFF_QK), g)


def mixer_inputs(h, w_in, q_g, k_g, use_rope):
    B, L, _ = h.shape
    z = h @ w_in
    u_a, v_a, q, k, v = jnp.split(z, [D_A, 2 * D_A, 2 * D_A + D_QK, 2 * D_A + 2 * D_QK], axis=-1)
    q = qk_heads(q, q_g)
    k = qk_heads(k, k_g)
    if use_rope:
        q = axial_rope(q)
        k = axial_rope(k)
    return u_a, v_a, q, k, v.reshape(B, L, DIFF_HEADS, DIFF_V)


def context_kv(hc, w_in, k_g):
    B, L, _ = hc.shape
    z = hc @ w_in[:, 2 * D_A + D_QK:]
    k, v = jnp.split(z, [D_QK], axis=-1)
    return qk_heads(k, k_g), v.reshape(B, L, DIFF_HEADS, DIFF_V)


def chunk_gmlp(u, v, ln_g, ws, bs):
    B, L, _ = u.shape
    u = jax.nn.gelu(u, approximate=False)
    v = group_layer_norm(jax.nn.gelu(v, approximate=False), ln_g)
    vc = v.reshape(B, L // CHUNK, CHUNK, GMLP_GROUPS, GMLP_CH)
    mixed = jnp.einsum('gpq,bnqgc->bnpgc', ws, vc) + bs.T[None, None, :, :, None]
    return u * mixed.reshape(B, L, D_A)


def diff_lambda(lq1, lk1, lq2, lk2, lam_init):
    f32 = jnp.float32
    return (jnp.exp(jnp.sum(lq1.astype(f32) * lk1.astype(f32)))
            - jnp.exp(jnp.sum(lq2.astype(f32) * lk2.astype(f32))) + lam_init)


def diff_attention(q, k, v, lam):
    B, Lq = q.shape[0], q.shape[1]
    n_blk = Lq // Q_BLOCK
    qb = jnp.moveaxis(q.reshape(B, n_blk, Q_BLOCK, DIFF_HEADS, 2, DIFF_QK), 1, 0)
    scale = DIFF_QK ** -0.5

    def block(q_blk):
        s = jnp.einsum('bqhcd,bkhcd->bhcqk', q_blk, k).astype(jnp.float32) * scale
        p = jax.nn.softmax(s, axis=-1)
        a = p[:, :, 0] - lam * p[:, :, 1]
        return jnp.einsum('bhqk,bkhd->bqhd', a.astype(v.dtype), v)

    out = lax.map(block, qb)
    return jnp.moveaxis(out, 0, 1).reshape(B, Lq, DIFF_HEADS, DIFF_V)


def merge_mixers(a_out, attn, subln_g, lam_init, w_out):
    B, L = a_out.shape[0], a_out.shape[1]
    attn = rms_norm(attn, subln_g) * (1.0 - lam_init)
    return jnp.concatenate([a_out, attn.reshape(B, L, D_V)], axis=-1) @ w_out


def peer_ffn(h, wq, keys, u_tab, v_tab):
    B, L, D = h.shape
    hb = h.reshape(B * L // PEER_BLOCK, PEER_BLOCK, D)

    def block(ht):
        q = (ht @ wq).reshape(PEER_BLOCK, PEER_HEADS, 2, PEER_DK // 2)
        s = jnp.einsum('thcd,hcnd->thcn', q, keys).astype(jnp.float32)
        s_a, i_a = lax.top_k(s[:, :, 0], PEER_TOPK)
        s_b, i_b = lax.top_k(s[:, :, 1], PEER_TOPK)
        cand_s = (s_a[..., :, None] + s_b[..., None, :]).reshape(PEER_BLOCK, PEER_HEADS, PEER_TOPK * PEER_TOPK)
        cand_i = (i_a[..., :, None] * N_KEYS + i_b[..., None, :]).reshape(PEER_BLOCK, PEER_HEADS, PEER_TOPK * PEER_TOPK)
        top_s, pos = lax.top_k(cand_s, PEER_TOPK)
        idx = jnp.take_along_axis(cand_i, pos, axis=-1)
        gate = jax.nn.softmax(top_s, axis=-1)
        u_e = jnp.take(u_tab, idx, axis=0)
        act = jax.nn.gelu(jnp.einsum('thkd,td->thk', u_e, ht).astype(jnp.float32), approximate=False)
        v_e = jnp.take(v_tab, idx, axis=0)
        return jnp.einsum('thk,thkd->td', (gate * act).astype(v_tab.dtype), v_e)

    return lax.map(block, hb).reshape(B, L, D)


def setup_inputs(seed: int = 0) -> dict:
    key = jax.random.key(seed)
    ks = jax.random.split(key, 26)
    f32 = jnp.float32

    def nrm(k, shape, scale):
        return jax.random.normal(k, shape, f32) * scale

    D = D_MODEL
    return {
        'x': nrm(ks[0], (BATCH, SEQ, D), 1.0),
        'c': nrm(ks[1], (BATCH, D), 1.0),
        'ctx': nrm(ks[2], (BATCH, CTX_LEN, D), 1.0),
        'c_ctx': nrm(ks[3], (D,), 1.0),
        'w_ada': nrm(ks[4], (DEPTH, D, 6 * D), D ** -0.5),
        'b_ada': nrm(ks[5], (DEPTH, 6 * D), 0.02),
        'norm1_g': 1.0 + nrm(ks[6], (DEPTH, D), 0.05),
        'norm2_g': 1.0 + nrm(ks[7], (DEPTH, D), 0.05),
        'w_in': nrm(ks[8], (DEPTH, D, D_IN), D ** -0.5),
        'gmlp_ln_g': 1.0 + nrm(ks[9], (DEPTH, D_A), 0.05),
        'gmlp_ws': nrm(ks[10], (DEPTH, GMLP_GROUPS, CHUNK, CHUNK), 0.5 * CHUNK ** -0.5),
        'gmlp_bs': 1.0 + nrm(ks[11], (DEPTH, GMLP_GROUPS, CHUNK), 0.1),
        'q_norm_g': 1.0 + nrm(ks[12], (DEPTH, DIFF_QK), 0.05),
        'k_norm_g': 1.0 + nrm(ks[13], (DEPTH, DIFF_QK), 0.05),
        'lambda_q1': nrm(ks[14], (DEPTH, DIFF_QK), 0.1),
        'lambda_k1': nrm(ks[15], (DEPTH, DIFF_QK), 0.1),
        'lambda_q2': nrm(ks[16], (DEPTH, DIFF_QK), 0.1),
        'lambda_k2': nrm(ks[17], (DEPTH, DIFF_QK), 0.1),
        'subln_g': 1.0 + nrm(ks[18], (DEPTH, DIFF_V), 0.05),
        'w_out': nrm(ks[19], (DEPTH, D_MIX, D), D_MIX ** -0.5),
        'peer_wq': nrm(ks[20], (DEPTH, D, PEER_HEADS * PEER_DK), D ** -0.5),
        'peer_keys': nrm(ks[21], (DEPTH, PEER_HEADS, 2, N_KEYS, PEER_DK // 2), (PEER_DK // 2) ** -0.5),
        'peer_u': nrm(ks[22], (DEPTH, N_EXPERTS, D), D ** -0.5),
        'peer_v': nrm(ks[23], (DEPTH, N_EXPERTS, D), 0.5),
    }


def reference(x, c, ctx, c_ctx, w_ada, b_ada, norm1_g, norm2_g, w_in, gmlp_ln_g, gmlp_ws, gmlp_bs,
              q_norm_g, k_norm_g, lambda_q1, lambda_k1, lambda_q2, lambda_k2, subln_g, w_out,
              peer_wq, peer_keys, peer_u, peer_v):
    for l in range(DEPTH):
        lam_init = 0.8 - 0.6 * math.exp(-0.3 * l)
        lam = diff_lambda(lambda_q1[l], lambda_k1[l], lambda_q2[l], lambda_k2[l], lam_init)
        sh1, sc1, g1, sh2, sc2, g2 = adaln(c, w_ada[l], b_ada[l])
        mods_c = adaln(c_ctx[None, :], w_ada[l], b_ada[l])

        hc = modulate(rms_norm(ctx, norm1_g[l]), mods_c[0], mods_c[1])
        if l < DEPTH - 1:
            u_c, va_c, q_c, k_c, v_c = mixer_inputs(hc, w_in[l], q_norm_g[l], k_norm_g[l], False)
        else:
            k_c, v_c = context_kv(hc, w_in[l], k_norm_g[l])

        h = modulate(rms_norm(x, norm1_g[l]), sh1, sc1)
        u_a, v_a, q, k, v = mixer_inputs(h, w_in[l], q_norm_g[l], k_norm_g[l], True)
        a_out = chunk_gmlp(u_a, v_a, gmlp_ln_g[l], gmlp_ws[l], gmlp_bs[l])
        attn = diff_attention(q, jnp.concatenate([k_c, k], axis=1), jnp.concatenate([v_c, v], axis=1), lam)
        x_new = x + g1 * merge_mixers(a_out, attn, subln_g[l], lam_init, w_out[l])

        h2 = modulate(rms_norm(x_new, norm2_g[l]), sh2, sc2)
        x_new = x_new + g2 * peer_ffn(h2, peer_wq[l], peer_keys[l], peer_u[l], peer_v[l])

        if l < DEPTH - 1:
            a_c = chunk_gmlp(u_c, va_c, gmlp_ln_g[l], gmlp_ws[l], gmlp_bs[l])
            attn_c = diff_attention(q_c, k_c, v_c, lam)
            ctx = ctx + mods_c[2] * merge_mixers(a_c, attn_c, subln_g[l], lam_init, w_out[l])
            hc2 = modulate(rms_norm(ctx, norm2_g[l]), mods_c[3], mods_c[4])
            ctx = ctx + mods_c[5] * peer_ffn(hc2, peer_wq[l], peer_keys[l], peer_u[l], peer_v[l])
        x = x_new
    return x
```

```python
import functools
import math

import jax
import jax.numpy as jnp
from jax import lax
from jax.experimental import pallas as pl
from jax.experimental.pallas import tpu as pltpu

F32 = jnp.float32
MXU_DTYPE = jnp.bfloat16

EPS = 1e-6
GRID_W = 64
ROPE_BASE = 10000.0

GMLP_GROUPS = 8
CHUNK = 128
DIFF_HEADS = 8
DIFF_QK = 64
DIFF_V = 128
PEER_HEADS = 8
N_KEYS = 128
PEER_TOPK = 16

LANES = 128
NEG_BIG = -1e30
POS_BIG = 1e30

VMEM_LIMIT = 56 * 1024 * 1024


def _cparams(sem):
    return pltpu.CompilerParams(dimension_semantics=sem, vmem_limit_bytes=VMEM_LIMIT)


def _gelu(x):
    return 0.5 * x * (1.0 + lax.erf(x * (2.0 ** -0.5)))


def _nt_dot(a, b):
    return lax.dot_general(a, b, (((1,), (1,)), ((), ())), preferred_element_type=F32)


def _ada_kernel(c_ref, w_ref, b_ref, o_ref):
    c = c_ref[...]
    s = c * jax.nn.sigmoid(c)
    o_ref[...] = jnp.dot(s.astype(MXU_DTYPE), w_ref[...].astype(MXU_DTYPE),
                         preferred_element_type=F32) + b_ref[...]


def _ada(cond8, w_ada, b_ada):
    d, n = w_ada.shape
    tn = 1024
    return pl.pallas_call(
        _ada_kernel,
        grid=(n // tn,),
        in_specs=[pl.BlockSpec((8, d), lambda j: (0, 0)),
                  pl.BlockSpec((d, tn), lambda j: (0, j)),
                  pl.BlockSpec((1, tn), lambda j: (0, j))],
        out_specs=pl.BlockSpec((8, tn), lambda j: (0, j)),
        out_shape=jax.ShapeDtypeStruct((8, n), F32),
        compiler_params=_cparams(("arbitrary",)), name="ada",
    )(cond8, w_ada, b_ada.reshape(1, n))


def _group_sum_64(xx, ones_bd):
    outs = []
    for b in range(xx.shape[1] // LANES):
        blk = xx[:, b * LANES:(b + 1) * LANES]
        hi = blk.astype(MXU_DTYPE)
        lo = (blk - hi.astype(F32)).astype(MXU_DTYPE)
        outs.append(jnp.dot(hi, ones_bd, preferred_element_type=F32)
                    + jnp.dot(lo, ones_bd, preferred_element_type=F32))
    return jnp.concatenate(outs, axis=1)


def _swap_pairs(x):
    n = x.shape[1]
    lane = lax.broadcasted_iota(jnp.int32, x.shape, 1)
    nxt = pltpu.roll(x, n - 1, 1)
    prv = pltpu.roll(x, 1, 1)
    return jnp.where((lane & 1) == 0, nxt, prv)


def _in_proj_kernel(sections, q_scale,
                    x_ref, ng_ref, sh_ref, sc_ref, w_ref, lng_ref, qg_ref, kg_ref,
                    cos_ref, sin_ref, ones_ref, o_ref, h_ref):
    j = pl.program_id(1)

    @pl.when(j == 0)
    def _():
        x = x_ref[...]
        y = x * lax.rsqrt(jnp.mean(x * x, axis=-1, keepdims=True) + EPS) * ng_ref[...]
        h_ref[...] = (y * (1.0 + sc_ref[...]) + sh_ref[...]).astype(h_ref.dtype)

    acc = jnp.dot(h_ref[...], w_ref[...], preferred_element_type=F32)
    tn = acc.shape[1]

    def qk_norm_rope(z, g_ref, scale):
        ms = _group_sum_64(z * z, ones_ref[...]) * (1.0 / DIFF_QK)
        y = z * lax.rsqrt(ms + EPS) * g_ref[...]
        reps = tn // LANES
        cos = jnp.concatenate([cos_ref[...]] * reps, axis=1)
        sin = jnp.concatenate([sin_ref[...]] * reps, axis=1)
        y = y * cos + _swap_pairs(y) * sin
        if scale != 1.0:
            y = y * scale
        return y

    for idx, kind in enumerate(sections):
        @pl.when(j == idx)
        def _(kind=kind):
            if kind == "u":
                o_ref[...] = _gelu(acc).astype(o_ref.dtype)
            elif kind == "va":
                g = _gelu(acc)
                for b in range(tn // CHUNK):
                    blk = g[:, b * CHUNK:(b + 1) * CHUNK]
                    mu = jnp.mean(blk, axis=-1, keepdims=True)
                    d = blk - mu
                    y = d * lax.rsqrt(jnp.mean(d * d, axis=-1, keepdims=True) + EPS)
                    y = y * lng_ref[:, b * CHUNK:(b + 1) * CHUNK]
                    o_ref[:, b * CHUNK:(b + 1) * CHUNK] = y.astype(o_ref.dtype)
            elif kind == "q":
                o_ref[...] = qk_norm_rope(acc, qg_ref, q_scale).astype(o_ref.dtype)
            elif kind == "k":
                o_ref[...] = qk_norm_rope(acc, kg_ref, 1.0).astype(o_ref.dtype)
            else:
                o_ref[...] = acc.astype(o_ref.dtype)


def _in_proj(x2, ng, sh, sc, w, lng, qg, kg, cos, sin, ones_bd, sections, q_scale, tm):
    l, d = x2.shape
    n = w.shape[1]
    tn = n // len(sections)
    row = lambda i, j: (0, 0)
    return pl.pallas_call(
        functools.partial(_in_proj_kernel, sections, q_scale),
        grid=(l // tm, len(sections)),
        in_specs=[pl.BlockSpec((tm, d), lambda i, j: (i, 0)),
                  pl.BlockSpec((1, d), row), pl.BlockSpec((1, d), row), pl.BlockSpec((1, d), row),
                  pl.BlockSpec((d, tn), lambda i, j: (0, j)),
                  pl.BlockSpec((1, tn), row), pl.BlockSpec((1, tn), row), pl.BlockSpec((1, tn), row),
                  pl.BlockSpec((tm, LANES), lambda i, j: (i, 0)),
                  pl.BlockSpec((tm, LANES), lambda i, j: (i, 0)),
                  pl.BlockSpec((LANES, LANES), row)],
        out_specs=pl.BlockSpec((tm, tn), lambda i, j: (i, j)),
        out_shape=jax.ShapeDtypeStruct((l, n), MXU_DTYPE),
        scratch_shapes=[pltpu.VMEM((tm, d), MXU_DTYPE)],
        compiler_params=_cparams(("arbitrary", "arbitrary")), name="in_proj",
    )(x2, ng, sh, sc, w, lng, qg, kg, cos, sin, ones_bd)


def _attn_kernel(tk, lam_init,
                 q_ref, k_ref, v_ref, kc_ref, vc_ref, g_ref, l1_ref, l2_ref, l3_ref, l4_ref,
                 o_ref):
    tq = q_ref.shape[0]
    q = q_ref[...]
    lane = lax.broadcasted_iota(jnp.int32, q.shape, 1)
    zero = jnp.zeros_like(q)
    qbd = jnp.concatenate([jnp.where(lane < DIFF_QK, q, zero),
                           jnp.where(lane >= DIFF_QK, q, zero)], axis=0)

    def step(kc, vc, carry):
        m, l, acc = carry
        s = _nt_dot(qbd, kc)
        m_new = jnp.maximum(m, jnp.max(s, axis=1, keepdims=True))
        p = jnp.exp2(s - m_new)
        alpha = jnp.exp2(m - m_new)
        l = alpha * l + jnp.sum(p, axis=1, keepdims=True)
        acc = alpha * acc + jnp.dot(p.astype(vc.dtype), vc, preferred_element_type=F32)
        return m_new, l, acc

    init = (jnp.full((2 * tq, 1), NEG_BIG, F32), jnp.zeros((2 * tq, 1), F32),
            jnp.zeros((2 * tq, DIFF_V), F32))
    carry = step(kc_ref[...], vc_ref[...], init)

    def body(c, carry):
        off = pl.multiple_of(c * tk, tk)
        return step(k_ref[pl.ds(off, tk), :], v_ref[pl.ds(off, tk), :], carry)

    m, l, acc = lax.fori_loop(0, k_ref.shape[0] // tk, body, carry)

    lam = (jnp.exp(jnp.sum(l1_ref[...] * l2_ref[...], axis=-1, keepdims=True))
           - jnp.exp(jnp.sum(l3_ref[...] * l4_ref[...], axis=-1, keepdims=True)) + lam_init)
    o = acc[:tq] / l[:tq] - lam * (acc[tq:] / l[tq:])
    y = o * lax.rsqrt(jnp.mean(o * o, axis=-1, keepdims=True) + EPS) * g_ref[...]
    o_ref[...] = (y * (1.0 - lam_init)).astype(o_ref.dtype)


def _attention(z, zc, subln_g, lq1, lk1, lq2, lk2, lam_init, tq, tk):
    l = z.shape[0]
    lc = zc.shape[0]
    nh = DIFF_HEADS
    qb, kb, vb = 2 * nh, 3 * nh, 4 * nh
    vec = lambda a: a.reshape(1, -1).astype(F32)
    cst = lambda h, i: (0, 0)
    return pl.pallas_call(
        functools.partial(_attn_kernel, tk, lam_init),
        grid=(nh, l // tq),
        in_specs=[pl.BlockSpec((tq, LANES), lambda h, i: (i, qb + h)),
                  pl.BlockSpec((l, LANES), lambda h, i: (0, kb + h)),
                  pl.BlockSpec((l, LANES), lambda h, i: (0, vb + h)),
                  pl.BlockSpec((lc, LANES), lambda h, i: (0, h)),
                  pl.BlockSpec((lc, LANES), lambda h, i: (0, nh + h)),
                  pl.BlockSpec((1, DIFF_V), cst),
                  pl.BlockSpec((1, DIFF_QK), cst), pl.BlockSpec((1, DIFF_QK), cst),
                  pl.BlockSpec((1, DIFF_QK), cst), pl.BlockSpec((1, DIFF_QK), cst)],
        out_specs=pl.BlockSpec((tq, DIFF_V), lambda h, i: (i, h)),
        out_shape=jax.ShapeDtypeStruct((l, nh * DIFF_V), MXU_DTYPE),
        compiler_params=_cparams(("arbitrary", "arbitrary")), name="attention",
    )(z, z, z, zc, zc, vec(subln_g), vec(lq1), vec(lk1), vec(lq2), vec(lk2))


def _gmlp_kernel(u_ref, v_ref, ws_ref, bs_ref, o_ref):
    tm = u_ref.shape[0]
    for c in range(tm // CHUNK):
        rows = slice(c * CHUNK, (c + 1) * CHUNK)
        for g in range(GMLP_GROUPS):
            cols = slice(g * CHUNK, (g + 1) * CHUNK)
            mixed = jnp.dot(ws_ref[g], v_ref[rows, cols], preferred_element_type=F32)
            mixed = mixed + bs_ref[:, g:g + 1]
            o_ref[rows, cols] = (u_ref[rows, cols].astype(F32) * mixed).astype(o_ref.dtype)


def _gmlp(z, ws, bs_t, tm):
    l = z.shape[0]
    da = GMLP_GROUPS * CHUNK
    return pl.pallas_call(
        _gmlp_kernel,
        grid=(l // tm,),
        in_specs=[pl.BlockSpec((tm, da), lambda i: (i, 0)),
                  pl.BlockSpec((tm, da), lambda i: (i, 1)),
                  pl.BlockSpec((GMLP_GROUPS, CHUNK, CHUNK), lambda i: (0, 0, 0)),
                  pl.BlockSpec((CHUNK, GMLP_GROUPS), lambda i: (0, 0))],
        out_specs=pl.BlockSpec((tm, da), lambda i: (i, 0)),
        out_shape=jax.ShapeDtypeStruct((l, da), MXU_DTYPE),
        compiler_params=_cparams(("arbitrary",)), name="gmlp",
    )(z, z, ws, bs_t)


def _out_proj_kernel(a_ref, t_ref, w_ref, x_ref, g1_ref, ng_ref, sh_ref, sc_ref, xn_ref, h2_ref):
    da = a_ref.shape[1]
    mix = (jnp.dot(a_ref[...], w_ref[0:da, :], preferred_element_type=F32)
           + jnp.dot(t_ref[...], w_ref[da:, :], preferred_element_type=F32))
    xn = x_ref[...] + g1_ref[...] * mix
    xn_ref[...] = xn
    y = xn * lax.rsqrt(jnp.mean(xn * xn, axis=-1, keepdims=True) + EPS) * ng_ref[...]
    h2_ref[...] = (y * (1.0 + sc_ref[...]) + sh_ref[...]).astype(h2_ref.dtype)


def _out_proj(a_out, attn, w_out, x2, g1, ng, sh, sc, tm):
    l, d = x2.shape
    da, dv = a_out.shape[1], attn.shape[1]
    cst = lambda i: (0, 0)
    return pl.pallas_call(
        _out_proj_kernel,
        grid=(l // tm,),
        in_specs=[pl.BlockSpec((tm, da), lambda i: (i, 0)),
                  pl.BlockSpec((tm, dv), lambda i: (i, 0)),
                  pl.BlockSpec((da + dv, d), cst),
                  pl.BlockSpec((tm, d), lambda i: (i, 0)),
                  pl.BlockSpec((1, d), cst), pl.BlockSpec((1, d), cst),
                  pl.BlockSpec((1, d), cst), pl.BlockSpec((1, d), cst)],
        out_specs=[pl.BlockSpec((tm, d), lambda i: (i, 0)),
                   pl.BlockSpec((tm, d), lambda i: (i, 0))],
        out_shape=[jax.ShapeDtypeStruct((l, d), F32), jax.ShapeDtypeStruct((l, d), MXU_DTYPE)],
        compiler_params=_cparams(("arbitrary",)), name="out_proj",
    )(a_out, attn, w_out, x2, g1, ng, sh, sc)


N_CAND = PEER_TOPK + 8 * (PEER_TOPK - 1)


def _peer_sel_kernel(h_ref, wq_ref, keys_ref, th_ref, ea_ref, bt_ref, eb_ref,
                     at_ref, as_ref, bs_ref, cand_ref):
    qp = jnp.dot(h_ref[...], wq_ref[...], preferred_element_type=F32).astype(MXU_DTYPE)
    nh = th_ref.shape[0]

    def top16(s, out_ref):
        for r in range(PEER_TOPK):
            m = jnp.max(s, axis=0, keepdims=True)
            out_ref[r:r + 1, :] = m
            if r < PEER_TOPK - 1:
                s = jnp.where(s == m, NEG_BIG, s)

    for h in range(nh):
        at_ref[...] = _nt_dot(keys_ref[2 * h], qp[:, (2 * h) * N_KEYS:(2 * h + 1) * N_KEYS])
        bt_ref[h] = _nt_dot(keys_ref[2 * h + 1], qp[:, (2 * h + 1) * N_KEYS:(2 * h + 2) * N_KEYS])
        top16(at_ref[...], as_ref)
        top16(bt_ref[h], bs_ref)
        cand_ref[0:PEER_TOPK, :] = as_ref[0:1, :] + bs_ref[...]
        for i in range(1, PEER_TOPK):
            r0 = PEER_TOPK + 8 * (i - 1)
            cand_ref[r0:r0 + 8, :] = as_ref[i:i + 1, :] + bs_ref[0:8, :]
        c = cand_ref[...]
        cmax = jnp.max(c, axis=0, keepdims=True)
        m = cmax
        zsum = jnp.ones_like(cmax)
        for r in range(1, PEER_TOPK):
            c = jnp.where(c == m, NEG_BIG, c)
            m = jnp.max(c, axis=0, keepdims=True)
            zsum = zsum + jnp.exp(m - cmax)
        tau = m
        a = at_ref[...]
        bs = bs_ref[...]
        th = jnp.full(a.shape, POS_BIG, F32)
        for r in range(PEER_TOPK):
            ar = as_ref[r:r + 1, :]
            th_r = jnp.min(jnp.where(ar + bs >= tau, bs, POS_BIG), axis=0, keepdims=True)
            th = jnp.where(a == ar, th_r, th)
        th_ref[h] = th
        ea_ref[h] = jnp.exp(a - as_ref[0:1, :])
        eb_ref[h] = jnp.exp(bt_ref[h] - bs_ref[0:1, :]) / zsum


def _peer_sel(h2, wq, keys, tm):
    l, d = h2.shape
    nh = keys.shape[0] // 2
    blk = pl.BlockSpec((nh, N_KEYS, tm), lambda i: (0, 0, i))
    shp = jax.ShapeDtypeStruct((nh, N_KEYS, l), F32)
    return pl.pallas_call(
        _peer_sel_kernel,
        grid=(l // tm,),
        in_specs=[pl.BlockSpec((tm, d), lambda i: (i, 0)),
                  pl.BlockSpec(wq.shape, lambda i: (0, 0)),
                  pl.BlockSpec(keys.shape, lambda i: (0, 0, 0))],
        out_specs=[blk, blk, blk, blk],
        out_shape=[shp, shp, shp, shp],
        scratch_shapes=[pltpu.VMEM((N_KEYS, tm), F32),
                        pltpu.VMEM((PEER_TOPK, tm), F32), pltpu.VMEM((PEER_TOPK, tm), F32),
                        pltpu.VMEM((N_CAND, tm), F32)],
        compiler_params=_cparams(("arbitrary",)), name="peer_sel",
    )(h2, wq, keys)


def _peer_dense_kernel(h_ref, u_ref, vt_ref, th_ref, ea_ref, bt_ref, eb_ref, x_ref, g2_ref, o_ref,
                       w_ref, acc_ref):
    e = pl.program_id(1)
    nh = th_ref.shape[0]
    ni = u_ref.shape[0] // N_KEYS

    @pl.when(e == 0)
    def _():
        acc_ref[...] = jnp.zeros_like(acc_ref)

    act = _gelu(_nt_dot(u_ref[...], h_ref[...]))
    for il in range(ni):
        i_glob = e * ni + il
        g = None
        for h in range(nh):
            th = th_ref[h, pl.ds(i_glob, 1), :]
            ea = ea_ref[h, pl.ds(i_glob, 1), :]
            t = jnp.where(bt_ref[h] >= th, eb_ref[h], 0.0) * ea
            g = t if g is None else g + t
        rows = slice(il * N_KEYS, (il + 1) * N_KEYS)
        w_ref[rows, :] = (g * act[rows, :]).astype(w_ref.dtype)
    acc_ref[...] += jnp.dot(vt_ref[...], w_ref[...], preferred_element_type=F32)

    @pl.when(e == pl.num_programs(1) - 1)
    def _():
        o_ref[...] = x_ref[...] + g2_ref[...] * acc_ref[...].T


def _peer_dense(h2, u, vt, th, ea, bt, eb, xn, g2, tt, te):
    l, d = h2.shape
    ne = u.shape[0]
    nh = th.shape[0]
    sel = pl.BlockSpec((nh, N_KEYS, tt), lambda i, e: (0, 0, i))
    return pl.pallas_call(
        _peer_dense_kernel,
        grid=(l // tt, ne // te),
        in_specs=[pl.BlockSpec((tt, d), lambda i, e: (i, 0)),
                  pl.BlockSpec((te, d), lambda i, e: (e, 0)),
                  pl.BlockSpec((d, te), lambda i, e: (0, e)),
                  sel, sel, sel, sel,
                  pl.BlockSpec((tt, d), lambda i, e: (i, 0)),
                  pl.BlockSpec((1, d), lambda i, e: (0, 0))],
        out_specs=pl.BlockSpec((tt, d), lambda i, e: (i, 0)),
        out_shape=jax.ShapeDtypeStruct((l, d), F32),
        scratch_shapes=[pltpu.VMEM((te, tt), MXU_DTYPE), pltpu.VMEM((d, tt), F32)],
        compiler_params=_cparams(("arbitrary", "arbitrary")), name="peer_dense",
    )(h2, u, vt, th, ea, bt, eb, xn, g2)


def _rope_tables(n_tok):
    n_rows = n_tok // GRID_W
    rows = jnp.repeat(jnp.arange(n_rows, dtype=F32), GRID_W)
    cols = jnp.tile(jnp.arange(GRID_W, dtype=F32), n_rows)
    n_freq = DIFF_QK // 4
    inv = ROPE_BASE ** (-jnp.arange(n_freq, dtype=F32) / n_freq)
    ang = jnp.concatenate([rows[:, None] * inv, cols[:, None] * inv], axis=-1)
    ang = jnp.repeat(ang, 2, axis=-1)
    sign = jnp.tile(jnp.array([-1.0, 1.0], F32), DIFF_QK // 2)
    cos = jnp.tile(jnp.cos(ang), (1, LANES // DIFF_QK))
    sin = jnp.tile(jnp.sin(ang) * sign, (1, LANES // DIFF_QK))
    return cos, sin


def kernel(x, c, ctx, c_ctx, w_ada, b_ada, norm1_g, norm2_g, w_in, gmlp_ln_g, gmlp_ws, gmlp_bs,
           q_norm_g, k_norm_g, lambda_q1, lambda_k1, lambda_q2, lambda_k2, subln_g, w_out,
           peer_wq, peer_keys, peer_u, peer_v):
    depth = w_ada.shape[0]
    assert depth == 1 and x.shape[0] == 1
    _, seq, d = x.shape
    lc = ctx.shape[1]
    da = GMLP_GROUPS * CHUNK
    dqk = DIFF_HEADS * 2 * DIFF_QK
    lam_init = 0.8 - 0.6 * math.exp(-0.3 * 0)
    row = lambda a: a.reshape(1, -1).astype(F32)

    cond8 = jnp.zeros((8, d), F32).at[0].set(c[0]).at[1].set(c_ctx)
    mod = _ada(cond8, w_ada[0], b_ada[0])
    sh1, sc1, g1, sh2, sc2, g2 = [mod[0:1, k * d:(k + 1) * d] for k in range(6)]
    sh1c, sc1c = mod[1:2, 0:d], mod[1:2, d:2 * d]

    w_in_b = w_in[0].astype(MXU_DTYPE)
    ones_bd = jnp.kron(jnp.eye(LANES // DIFF_QK, dtype=F32),
                       jnp.ones((DIFF_QK, DIFF_QK), F32)).astype(MXU_DTYPE)
    qg = jnp.tile(row(q_norm_g[0]), (1, dqk // DIFF_QK))
    kg = jnp.tile(row(k_norm_g[0]), (1, dqk // DIFF_QK))
    lng = row(gmlp_ln_g[0])
    cos, sin = _rope_tables(seq)
    q_scale = (DIFF_QK ** -0.5) * math.log2(math.e)

    z = _in_proj(x[0], row(norm1_g[0]), sh1, sc1, w_in_b, lng, qg, kg, cos, sin, ones_bd,
                 ("u", "va", "q", "k", "v"), q_scale, tm=min(512, seq))
    zc = _in_proj(ctx[0], row(norm1_g[0]), sh1c, sc1c, w_in_b[:, 2 * da + dqk:], lng, qg, kg,
                  jnp.ones((lc, LANES), F32), jnp.zeros((lc, LANES), F32), ones_bd,
                  ("k", "v"), 1.0, tm=lc)

    attn = _attention(z, zc, subln_g[0], lambda_q1[0], lambda_k1[0], lambda_q2[0], lambda_k2[0],
                      lam_init, tq=min(256, seq), tk=min(512, seq))
    a_out = _gmlp(z, gmlp_ws[0].astype(MXU_DTYPE), gmlp_bs[0].T.astype(F32), tm=min(512, seq))
    xn, h2 = _out_proj(a_out, attn, w_out[0].astype(MXU_DTYPE), x[0], g1, row(norm2_g[0]),
                       sh2, sc2, tm=min(256, seq))

    keys = peer_keys[0].reshape(PEER_HEADS * 2, N_KEYS, -1).astype(MXU_DTYPE)
    th, ea, bt, eb = _peer_sel(h2, peer_wq[0].astype(MXU_DTYPE), keys, tm=min(512, seq))
    out = _peer_dense(h2, peer_u[0].astype(MXU_DTYPE), peer_v[0].T.astype(MXU_DTYPE),
                      th, ea, bt, eb, xn, g2, tt=min(512, seq), te=512)
    return out[None]
```

```python
import functools
import math

import jax
import jax.numpy as jnp
from jax import lax
from jax.experimental import pallas as pl
from jax.experimental.pallas import tpu as pltpu

F32 = jnp.float32
MXU_DTYPE = jnp.bfloat16

EPS = 1e-6
GRID_W = 64
ROPE_BASE = 10000.0

GMLP_GROUPS = 8
CHUNK = 128
DIFF_HEADS = 8
DIFF_QK = 64
DIFF_V = 128
PEER_HEADS = 8
N_KEYS = 128
PEER_TOPK = 16

LANES = 128
NEG_BIG = -1e30
POS_BIG = 1e30

VMEM_LIMIT = 56 * 1024 * 1024


def _cparams(sem):
    return pltpu.CompilerParams(dimension_semantics=sem, vmem_limit_bytes=VMEM_LIMIT)


def _gelu(x):
    return 0.5 * x * (1.0 + lax.erf(x * (2.0 ** -0.5)))


def _ada_kernel(c_ref, w_ref, b_ref, o_ref):
    c = c_ref[...]
    s = c * jax.nn.sigmoid(c)
    o_ref[...] = jnp.dot(s.astype(MXU_DTYPE), w_ref[...].astype(MXU_DTYPE),
                         preferred_element_type=F32) + b_ref[...]


def _ada(cond8, w_ada, b_ada):
    d, n = w_ada.shape
    tn = 1024
    return pl.pallas_call(
        _ada_kernel,
        grid=(n // tn,),
        in_specs=[pl.BlockSpec((8, d), lambda j: (0, 0)),
                  pl.BlockSpec((d, tn), lambda j: (0, j)),
                  pl.BlockSpec((1, tn), lambda j: (0, j))],
        out_specs=pl.BlockSpec((8, tn), lambda j: (0, j)),
        out_shape=jax.ShapeDtypeStruct((8, n), F32),
        compiler_params=_cparams(("arbitrary",)), name="ada",
    )(cond8, w_ada, b_ada.reshape(1, n))


def _group_sum_64(xx, ones_bd):
    outs = []
    for b in range(xx.shape[1] // LANES):
        blk = xx[:, b * LANES:(b + 1) * LANES]
        hi = blk.astype(MXU_DTYPE)
        lo = (blk - hi.astype(F32)).astype(MXU_DTYPE)
        outs.append(jnp.dot(hi, ones_bd, preferred_element_type=F32)
                    + jnp.dot(lo, ones_bd, preferred_element_type=F32))
    return jnp.concatenate(outs, axis=1)


def _swap_pairs(x):
    n = x.shape[1]
    lane = lax.broadcasted_iota(jnp.int32, x.shape, 1)
    nxt = pltpu.roll(x, n - 1, 1)
    prv = pltpu.roll(x, 1, 1)
    return jnp.where((lane & 1) == 0, nxt, prv)


def _in_proj_kernel(sections, q_scale,
                    x_ref, ng_ref, sh_ref, sc_ref, w_ref, lng_ref, qg_ref, kg_ref,
                    cos_ref, sin_ref, ones_ref, o_ref, vt_ref, h_ref):
    j = pl.program_id(1)

    @pl.when(j == 0)
    def _():
        x = x_ref[...]
        y = x * lax.rsqrt(jnp.mean(x * x, axis=-1, keepdims=True) + EPS) * ng_ref[...]
        h_ref[...] = (y * (1.0 + sc_ref[...]) + sh_ref[...]).astype(h_ref.dtype)

    acc = jnp.dot(h_ref[...], w_ref[...], preferred_element_type=F32)
    tn = acc.shape[1]

    def qk_norm_rope(z, g_ref, scale):
        ms = _group_sum_64(z * z, ones_ref[...]) * (1.0 / DIFF_QK)
        y = z * lax.rsqrt(ms + EPS) * g_ref[...]
        reps = tn // LANES
        cos = jnp.concatenate([cos_ref[...]] * reps, axis=1)
        sin = jnp.concatenate([sin_ref[...]] * reps, axis=1)
        y = y * cos + _swap_pairs(y) * sin
        if scale != 1.0:
            y = y * scale
        return y

    for idx, kind in enumerate(sections):
        @pl.when(j == idx)
        def _(kind=kind):
            if kind == "u":
                o_ref[...] = _gelu(acc).astype(o_ref.dtype)
            elif kind == "va":
                g = _gelu(acc)
                for b in range(tn // CHUNK):
                    blk = g[:, b * CHUNK:(b + 1) * CHUNK]
                    mu = jnp.mean(blk, axis=-1, keepdims=True)
                    d = blk - mu
                    y = d * lax.rsqrt(jnp.mean(d * d, axis=-1, keepdims=True) + EPS)
                    y = y * lng_ref[:, b * CHUNK:(b + 1) * CHUNK]
                    o_ref[:, b * CHUNK:(b + 1) * CHUNK] = y.astype(o_ref.dtype)
            elif kind == "q":
                o_ref[...] = qk_norm_rope(acc, qg_ref, q_scale).astype(o_ref.dtype)
            elif kind == "k":
                o_ref[...] = qk_norm_rope(acc, kg_ref, 1.0).astype(o_ref.dtype)
            else:
                vt_ref[...] = acc.T.astype(vt_ref.dtype)


def _in_proj(x2, ng, sh, sc, w, lng, qg, kg, cos, sin, ones_bd, sections, q_scale, tm):
    l, d = x2.shape
    n = w.shape[1]
    n_sec = len(sections)
    tn = n // n_sec
    assert sections[-1] == "v"
    row = lambda i, j: (0, 0)
    z_col = lambda j: jnp.minimum(j, n_sec - 2)
    return pl.pallas_call(
        functools.partial(_in_proj_kernel, sections, q_scale),
        grid=(l // tm, n_sec),
        in_specs=[pl.BlockSpec((tm, d), lambda i, j: (i, 0)),
                  pl.BlockSpec((1, d), row), pl.BlockSpec((1, d), row), pl.BlockSpec((1, d), row),
                  pl.BlockSpec((d, tn), lambda i, j: (0, j)),
                  pl.BlockSpec((1, tn), row), pl.BlockSpec((1, tn), row), pl.BlockSpec((1, tn), row),
                  pl.BlockSpec((tm, LANES), lambda i, j: (i, 0)),
                  pl.BlockSpec((tm, LANES), lambda i, j: (i, 0)),
                  pl.BlockSpec((LANES, LANES), row)],
        out_specs=[pl.BlockSpec((tm, tn), lambda i, j: (i, z_col(j))),
                   pl.BlockSpec((tn, tm), lambda i, j: (0, i))],
        out_shape=[jax.ShapeDtypeStruct((l, n - tn), MXU_DTYPE),
                   jax.ShapeDtypeStruct((tn, l), MXU_DTYPE)],
        scratch_shapes=[pltpu.VMEM((tm, d), MXU_DTYPE)],
        compiler_params=_cparams(("arbitrary", "arbitrary")), name="in_proj",
    )(x2, ng, sh, sc, w, lng, qg, kg, cos, sin, ones_bd)


def _attn_kernel(tk, lam_init,
                 q_ref, k_ref, vt_ref, kc_ref, vct_ref, g_ref, l1_ref, l2_ref, l3_ref, l4_ref,
                 o_ref, sa_ref, sb_ref, pa_ref, pb_ref):
    tq = q_ref.shape[0]
    qt = q_ref[...].astype(F32).T.astype(MXU_DTYPE)
    sub = lax.broadcasted_iota(jnp.int32, qt.shape, 0)
    zero = jnp.zeros_like(qt)
    qbd = jnp.concatenate([jnp.where(sub < DIFF_QK, qt, zero),
                           jnp.where(sub >= DIFF_QK, qt, zero)], axis=1)
    n_chunks = k_ref.shape[0] // tk

    def scores(c, s_ref):
        kc = k_ref[pl.ds(pl.multiple_of(c * tk, tk), tk), :]
        s_ref[...] = jnp.dot(kc, qbd, preferred_element_type=F32)

    def softmax(s_ref, p_ref, state):
        m, l, _, acc = state
        s = s_ref[...]
        m_new = jnp.maximum(m, jnp.max(s, axis=0, keepdims=True))
        p = jnp.exp2(s - m_new)
        alpha = jnp.exp2(m - m_new)
        l = alpha * l + jnp.sum(p, axis=0, keepdims=True)
        p_ref[...] = p.astype(p_ref.dtype)
        return m_new, l, alpha, acc

    def values(c, p_ref, state):
        m, l, alpha, acc = state
        vt = vt_ref[:, pl.ds(pl.multiple_of(c * tk, tk), tk)]
        return m, l, alpha, alpha * acc + jnp.dot(vt, p_ref[...], preferred_element_type=F32)

    s = jnp.dot(kc_ref[...], qbd, preferred_element_type=F32)
    m = jnp.max(s, axis=0, keepdims=True)
    p = jnp.exp2(s - m)
    state = (m, jnp.sum(p, axis=0, keepdims=True), jnp.ones_like(m),
             jnp.dot(vct_ref[...], p.astype(MXU_DTYPE), preferred_element_type=F32))

    scores(0, sa_ref)
    if n_chunks == 1:
        state = values(0, pa_ref, softmax(sa_ref, pa_ref, state))
    else:
        assert n_chunks % 2 == 0
        scores(1, sb_ref)
        state = softmax(sa_ref, pa_ref, state)

        def body(j, state):
            c = 2 * j + 1
            scores(c + 1, sa_ref)
            state = softmax(sb_ref, pb_ref, values(c - 1, pa_ref, state))
            scores(c + 2, sb_ref)
            return softmax(sa_ref, pa_ref, values(c, pb_ref, state))

        state = lax.fori_loop(0, n_chunks // 2 - 1, body, state)
        state = softmax(sb_ref, pb_ref, values(n_chunks - 2, pa_ref, state))
        state = values(n_chunks - 1, pb_ref, state)
    _, l, _, acc = state

    lam = (jnp.exp(jnp.sum(l1_ref[...] * l2_ref[...], axis=-1, keepdims=True))
           - jnp.exp(jnp.sum(l3_ref[...] * l4_ref[...], axis=-1, keepdims=True)) + lam_init)
    o = acc[:, :tq] / l[:, :tq] - lam * (acc[:, tq:] / l[:, tq:])
    y = o * lax.rsqrt(jnp.mean(o * o, axis=0, keepdims=True) + EPS) * g_ref[...]
    o_ref[...] = (y * (1.0 - lam_init)).T.astype(o_ref.dtype)


def _attention(z, vt, zc, vct, subln_g, lq1, lk1, lq2, lk2, lam_init, tq, tk):
    l = z.shape[0]
    lc = zc.shape[0]
    nh = DIFF_HEADS
    qb, kb = 2 * nh, 3 * nh
    vec = lambda a: a.reshape(1, -1).astype(F32)
    cst = lambda h, i: (0, 0)
    return pl.pallas_call(
        functools.partial(_attn_kernel, tk, lam_init),
        grid=(nh, l // tq),
        in_specs=[pl.BlockSpec((tq, LANES), lambda h, i: (i, qb + h)),
                  pl.BlockSpec((l, LANES), lambda h, i: (0, kb + h)),
                  pl.BlockSpec((DIFF_V, l), lambda h, i: (h, 0)),
                  pl.BlockSpec((lc, LANES), lambda h, i: (0, h)),
                  pl.BlockSpec((DIFF_V, lc), lambda h, i: (h, 0)),
                  pl.BlockSpec((DIFF_V, 1), cst),
                  pl.BlockSpec((1, DIFF_QK), cst), pl.BlockSpec((1, DIFF_QK), cst),
                  pl.BlockSpec((1, DIFF_QK), cst), pl.BlockSpec((1, DIFF_QK), cst)],
        out_specs=pl.BlockSpec((tq, DIFF_V), lambda h, i: (i, h)),
        out_shape=jax.ShapeDtypeStruct((l, nh * DIFF_V), MXU_DTYPE),
        scratch_shapes=[pltpu.VMEM((tk, 2 * tq), F32), pltpu.VMEM((tk, 2 * tq), F32),
                        pltpu.VMEM((tk, 2 * tq), MXU_DTYPE), pltpu.VMEM((tk, 2 * tq), MXU_DTYPE)],
        compiler_params=_cparams(("arbitrary", "arbitrary")), name="attention",
    )(z, z, vt, zc, vct, subln_g.reshape(-1, 1).astype(F32), vec(lq1), vec(lk1), vec(lq2), vec(lk2))


def _gmlp_kernel(u_ref, v_ref, ws_ref, bs_ref, o_ref):
    tm = u_ref.shape[0]
    for c in range(tm // CHUNK):
        rows = slice(c * CHUNK, (c + 1) * CHUNK)
        for g in range(GMLP_GROUPS):
            cols = slice(g * CHUNK, (g + 1) * CHUNK)
            mixed = jnp.dot(ws_ref[g], v_ref[rows, cols], preferred_element_type=F32)
            mixed = mixed + bs_ref[:, g:g + 1]
            o_ref[rows, cols] = (u_ref[rows, cols].astype(F32) * mixed).astype(o_ref.dtype)


def _gmlp(z, ws, bs_t, tm):
    l = z.shape[0]
    da = GMLP_GROUPS * CHUNK
    return pl.pallas_call(
        _gmlp_kernel,
        grid=(l // tm,),
        in_specs=[pl.BlockSpec((tm, da), lambda i: (i, 0)),
                  pl.BlockSpec((tm, da), lambda i: (i, 1)),
                  pl.BlockSpec((GMLP_GROUPS, CHUNK, CHUNK), lambda i: (0, 0, 0)),
                  pl.BlockSpec((CHUNK, GMLP_GROUPS), lambda i: (0, 0))],
        out_specs=pl.BlockSpec((tm, da), lambda i: (i, 0)),
        out_shape=jax.ShapeDtypeStruct((l, da), MXU_DTYPE),
        compiler_params=_cparams(("arbitrary",)), name="gmlp",
    )(z, z, ws, bs_t)


def _out_proj_kernel(a_ref, t_ref, w_ref, x_ref, g1_ref, ng_ref, sh_ref, sc_ref, xn_ref, h2_ref):
    da = a_ref.shape[1]
    mix = (jnp.dot(a_ref[...], w_ref[0:da, :], preferred_element_type=F32)
           + jnp.dot(t_ref[...], w_ref[da:, :], preferred_element_type=F32))
    xn = x_ref[...] + g1_ref[...] * mix
    xn_ref[...] = xn
    y = xn * lax.rsqrt(jnp.mean(xn * xn, axis=-1, keepdims=True) + EPS) * ng_ref[...]
    h2_ref[...] = (y * (1.0 + sc_ref[...]) + sh_ref[...]).T.astype(h2_ref.dtype)


def _out_proj(a_out, attn, w_out, x2, g1, ng, sh, sc, tm):
    l, d = x2.shape
    da, dv = a_out.shape[1], attn.shape[1]
    cst = lambda i: (0, 0)
    return pl.pallas_call(
        _out_proj_kernel,
        grid=(l // tm,),
        in_specs=[pl.BlockSpec((tm, da), lambda i: (i, 0)),
                  pl.BlockSpec((tm, dv), lambda i: (i, 0)),
                  pl.BlockSpec((da + dv, d), cst),
                  pl.BlockSpec((tm, d), lambda i: (i, 0)),
                  pl.BlockSpec((1, d), cst), pl.BlockSpec((1, d), cst),
                  pl.BlockSpec((1, d), cst), pl.BlockSpec((1, d), cst)],
        out_specs=[pl.BlockSpec((tm, d), lambda i: (i, 0)),
                   pl.BlockSpec((d, tm), lambda i: (0, i))],
        out_shape=[jax.ShapeDtypeStruct((l, d), F32), jax.ShapeDtypeStruct((d, l), MXU_DTYPE)],
        compiler_params=_cparams(("arbitrary",)), name="out_proj",
    )(a_out, attn, w_out, x2, g1, ng, sh, sc)


N_CAND = PEER_TOPK + 8 * (PEER_TOPK - 1)


def _peer_sel_kernel(ht_ref, wqt_ref, keys_ref, th_ref, ea_ref, bt_ref, eb_ref,
                     qp_ref, at_ref, as_ref, bs_ref, cand_ref):
    tm = ht_ref.shape[1]
    qp_ref[...] = jnp.dot(wqt_ref[...], ht_ref[...],
                          preferred_element_type=F32).astype(qp_ref.dtype)
    nh = th_ref.shape[0]

    def top16(s, out_ref):
        for r in range(PEER_TOPK):
            m = jnp.max(s, axis=0, keepdims=True)
            out_ref[r:r + 1, :] = m
            if r < PEER_TOPK - 1:
                s = jnp.where(s == m, NEG_BIG, s)

    def strip(h, c0):
        cols = pl.ds(c0, LANES)
        a = at_ref[:, cols]
        b = bt_ref[h, :, cols]
        top16(a, as_ref)
        top16(b, bs_ref)
        cand_ref[0:PEER_TOPK, :] = as_ref[0:1, :] + bs_ref[...]
        for i in range(1, PEER_TOPK):
            r0 = PEER_TOPK + 8 * (i - 1)
            cand_ref[r0:r0 + 8, :] = as_ref[i:i + 1, :] + bs_ref[0:8, :]
        c = cand_ref[...]
        cmax = jnp.max(c, axis=0, keepdims=True)
        m = cmax
        zsum = jnp.ones_like(cmax)
        for r in range(1, PEER_TOPK):
            c = jnp.where(c == m, NEG_BIG, c)
            m = jnp.max(c, axis=0, keepdims=True)
            zsum = zsum + jnp.exp(m - cmax)
        tau = m
        bs = bs_ref[...]
        th = jnp.full(a.shape, POS_BIG, F32)
        for r in range(PEER_TOPK):
            ar = as_ref[r:r + 1, :]
            th_r = jnp.min(jnp.where(ar + bs >= tau, bs, POS_BIG), axis=0, keepdims=True)
            th = jnp.where(a == ar, th_r, th)
        th_ref[h, :, cols] = th
        ea_ref[h, :, cols] = jnp.exp(a - as_ref[0:1, :])
        eb_ref[h, :, cols] = jnp.exp(b - bs_ref[0:1, :]) / zsum

    def head(h, carry):
        ra = pl.multiple_of(2 * h * N_KEYS, N_KEYS)
        rb = pl.multiple_of((2 * h + 1) * N_KEYS, N_KEYS)
        at_ref[...] = jnp.dot(keys_ref[2 * h], qp_ref[pl.ds(ra, N_KEYS), :],
                              preferred_element_type=F32)
        bt_ref[h] = jnp.dot(keys_ref[2 * h + 1], qp_ref[pl.ds(rb, N_KEYS), :],
                            preferred_element_type=F32)

        def body(s, carry):
            strip(h, pl.multiple_of(s * LANES, LANES))
            return carry

        return lax.fori_loop(0, tm // LANES, body, carry)

    lax.fori_loop(0, nh, head, 0)


def _peer_sel(h2t, wqt, keys, tm):
    d, l = h2t.shape
    nh = keys.shape[0] // 2
    blk = pl.BlockSpec((nh, N_KEYS, tm), lambda i: (0, 0, i))
    shp = jax.ShapeDtypeStruct((nh, N_KEYS, l), F32)
    return pl.pallas_call(
        _peer_sel_kernel,
        grid=(l // tm,),
        in_specs=[pl.BlockSpec((d, tm), lambda i: (0, i)),
                  pl.BlockSpec(wqt.shape, lambda i: (0, 0)),
                  pl.BlockSpec(keys.shape, lambda i: (0, 0, 0))],
        out_specs=[blk, blk, blk, blk],
        out_shape=[shp, shp, shp, shp],
        scratch_shapes=[pltpu.VMEM((wqt.shape[0], tm), MXU_DTYPE),
                        pltpu.VMEM((N_KEYS, tm), F32),
                        pltpu.VMEM((PEER_TOPK, LANES), F32), pltpu.VMEM((PEER_TOPK, LANES), F32),
                        pltpu.VMEM((N_CAND, LANES), F32)],
        compiler_params=_cparams(("arbitrary",)), name="peer_sel",
    )(h2t, wqt, keys)


def _peer_dense_kernel(ht_ref, u_ref, vt_ref, th_ref, ea_ref, bt_ref, eb_ref, x_ref, g2_ref, o_ref,
                       act_ref, acc_ref):
    e = pl.program_id(1)
    n_tiles = pl.num_programs(1) - 1
    nh = th_ref.shape[0]
    ni = u_ref.shape[0] // N_KEYS

    @pl.when(e == 0)
    def _():
        acc_ref[...] = jnp.zeros_like(acc_ref)
        act_ref[...] = jnp.dot(u_ref[...], ht_ref[...], preferred_element_type=F32)

    @pl.when(e > 0)
    def _():
        act = _gelu(act_ref[...])
        act_ref[...] = jnp.dot(u_ref[...], ht_ref[...], preferred_element_type=F32)
        ws = []
        for il in range(ni):
            i_glob = (e - 1) * ni + il
            g = None
            for h in range(nh):
                th = th_ref[h, pl.ds(i_glob, 1), :]
                ea = ea_ref[h, pl.ds(i_glob, 1), :]
                t = jnp.where(bt_ref[h] >= th, eb_ref[h], 0.0) * ea
                g = t if g is None else g + t
            ws.append((g * act[il * N_KEYS:(il + 1) * N_KEYS, :]).astype(vt_ref.dtype))
        w = jnp.concatenate(ws, axis=0)
        acc_ref[...] += jnp.dot(vt_ref[...], w, preferred_element_type=F32)

    @pl.when(e == n_tiles)
    def _():
        o_ref[...] = x_ref[...] + g2_ref[...] * acc_ref[...].T


def _peer_dense(h2t, u, vt, th, ea, bt, eb, xn, g2, tt, te):
    d, l = h2t.shape
    n_tiles = u.shape[0] // te
    nh = th.shape[0]
    sel = pl.BlockSpec((nh, N_KEYS, tt), lambda i, e: (0, 0, i))
    return pl.pallas_call(
        _peer_dense_kernel,
        grid=(l // tt, n_tiles + 1),
        in_specs=[pl.BlockSpec((d, tt), lambda i, e: (0, i)),
                  pl.BlockSpec((te, d), lambda i, e: (jnp.minimum(e, n_tiles - 1), 0)),
                  pl.BlockSpec((d, te), lambda i, e: (0, jnp.maximum(e - 1, 0))),
                  sel, sel, sel, sel,
                  pl.BlockSpec((tt, d), lambda i, e: (i, 0)),
                  pl.BlockSpec((1, d), lambda i, e: (0, 0))],
        out_specs=pl.BlockSpec((tt, d), lambda i, e: (i, 0)),
        out_shape=jax.ShapeDtypeStruct((l, d), F32),
        scratch_shapes=[pltpu.VMEM((te, tt), F32), pltpu.VMEM((d, tt), F32)],
        compiler_params=_cparams(("arbitrary", "arbitrary")), name="peer_dense",
    )(h2t, u, vt, th, ea, bt, eb, xn, g2)


def _rope_tables(n_tok):
    n_rows = n_tok // GRID_W
    rows = jnp.repeat(jnp.arange(n_rows, dtype=F32), GRID_W)
    cols = jnp.tile(jnp.arange(GRID_W, dtype=F32), n_rows)
    n_freq = DIFF_QK // 4
    inv = ROPE_BASE ** (-jnp.arange(n_freq, dtype=F32) / n_freq)
    ang = jnp.concatenate([rows[:, None] * inv, cols[:, None] * inv], axis=-1)
    ang = jnp.repeat(ang, 2, axis=-1)
    sign = jnp.tile(jnp.array([-1.0, 1.0], F32), DIFF_QK // 2)
    cos = jnp.tile(jnp.cos(ang), (1, LANES // DIFF_QK))
    sin = jnp.tile(jnp.sin(ang) * sign, (1, LANES // DIFF_QK))
    return cos, sin


def kernel(x, c, ctx, c_ctx, w_ada, b_ada, norm1_g, norm2_g, w_in, gmlp_ln_g, gmlp_ws, gmlp_bs,
           q_norm_g, k_norm_g, lambda_q1, lambda_k1, lambda_q2, lambda_k2, subln_g, w_out,
           peer_wq, peer_keys, peer_u, peer_v):
    depth = w_ada.shape[0]
    assert depth == 1 and x.shape[0] == 1
    _, seq, d = x.shape
    lc = ctx.shape[1]
    da = GMLP_GROUPS * CHUNK
    dqk = DIFF_HEADS * 2 * DIFF_QK
    lam_init = 0.8 - 0.6 * math.exp(-0.3 * 0)
    row = lambda a: a.reshape(1, -1).astype(F32)

    cond8 = jnp.zeros((8, d), F32).at[0].set(c[0]).at[1].set(c_ctx)
    mod = _ada(cond8, w_ada[0], b_ada[0])
    sh1, sc1, g1, sh2, sc2, g2 = [mod[0:1, k * d:(k + 1) * d] for k in range(6)]
    sh1c, sc1c = mod[1:2, 0:d], mod[1:2, d:2 * d]

    w_in_b = w_in[0].astype(MXU_DTYPE)
    ones_bd = jnp.kron(jnp.eye(LANES // DIFF_QK, dtype=F32),
                       jnp.ones((DIFF_QK, DIFF_QK), F32)).astype(MXU_DTYPE)
    qg = jnp.tile(row(q_norm_g[0]), (1, dqk // DIFF_QK))
    kg = jnp.tile(row(k_norm_g[0]), (1, dqk // DIFF_QK))
    lng = row(gmlp_ln_g[0])
    cos, sin = _rope_tables(seq)
    q_scale = (DIFF_QK ** -0.5) * math.log2(math.e)

    z, vt = _in_proj(x[0], row(norm1_g[0]), sh1, sc1, w_in_b, lng, qg, kg, cos, sin, ones_bd,
                     ("u", "va", "q", "k", "v"), q_scale, tm=min(512, seq))
    zc, vct = _in_proj(ctx[0], row(norm1_g[0]), sh1c, sc1c, w_in_b[:, 2 * da + dqk:], lng, qg, kg,
                       jnp.ones((lc, LANES), F32), jnp.zeros((lc, LANES), F32), ones_bd,
                       ("k", "v"), 1.0, tm=lc)

    attn = _attention(z, vt, zc, vct, subln_g[0], lambda_q1[0], lambda_k1[0], lambda_q2[0],
                      lambda_k2[0], lam_init, tq=min(256, seq), tk=min(2048, seq))
    a_out = _gmlp(z, gmlp_ws[0].astype(MXU_DTYPE), gmlp_bs[0].T.astype(F32), tm=min(512, seq))
    xn, h2t = _out_proj(a_out, attn, w_out[0].astype(MXU_DTYPE), x[0], g1, row(norm2_g[0]),
                        sh2, sc2, tm=min(256, seq))

    keys = peer_keys[0].reshape(PEER_HEADS * 2, N_KEYS, -1).astype(MXU_DTYPE)
    th, ea, bt, eb = _peer_sel(h2t, peer_wq[0].T.astype(MXU_DTYPE), keys, tm=min(512, seq))
    out = _peer_dense(h2t, peer_u[0].astype(MXU_DTYPE), peer_v[0].T.astype(MXU_DTYPE),
                      th, ea, bt, eb, xn, g2, tt=min(512, seq), te=512)
    return out[None]
```

```python
import functools
import math

import jax
import jax.numpy as jnp
from jax import lax
from jax.experimental import pallas as pl
from jax.experimental.pallas import tpu as pltpu

F32 = jnp.float32
MXU_DTYPE = jnp.bfloat16

EPS = 1e-6
GRID_W = 64
ROPE_BASE = 10000.0

GMLP_GROUPS = 8
CHUNK = 128
DIFF_HEADS = 8
DIFF_QK = 64
DIFF_V = 128
PEER_HEADS = 8
N_KEYS = 128
PEER_TOPK = 16

LANES = 128
NEG_BIG = -1e30
POS_BIG = 1e30

VMEM_LIMIT = 56 * 1024 * 1024


def _cparams(sem):
    return pltpu.CompilerParams(dimension_semantics=sem, vmem_limit_bytes=VMEM_LIMIT)


def _gelu(x):
    return 0.5 * x * (1.0 + lax.erf(x * (2.0 ** -0.5)))


def _ada_kernel(c_ref, w_ref, b_ref, o_ref):
    c = c_ref[...]
    s = c * jax.nn.sigmoid(c)
    o_ref[...] = jnp.dot(s.astype(MXU_DTYPE), w_ref[...].astype(MXU_DTYPE),
                         preferred_element_type=F32) + b_ref[...]


def _ada(cond8, w_ada, b_ada):
    d, n = w_ada.shape
    tn = 1024
    return pl.pallas_call(
        _ada_kernel,
        grid=(n // tn,),
        in_specs=[pl.BlockSpec((8, d), lambda j: (0, 0)),
                  pl.BlockSpec((d, tn), lambda j: (0, j)),
                  pl.BlockSpec((1, tn), lambda j: (0, j))],
        out_specs=pl.BlockSpec((8, tn), lambda j: (0, j)),
        out_shape=jax.ShapeDtypeStruct((8, n), F32),
        compiler_params=_cparams(("arbitrary",)), name="ada",
    )(cond8, w_ada, b_ada.reshape(1, n))


def _group_sum_64(xx, ones_bd):
    outs = []
    for b in range(xx.shape[1] // LANES):
        blk = xx[:, b * LANES:(b + 1) * LANES]
        hi = blk.astype(MXU_DTYPE)
        lo = (blk - hi.astype(F32)).astype(MXU_DTYPE)
        outs.append(jnp.dot(hi, ones_bd, preferred_element_type=F32)
                    + jnp.dot(lo, ones_bd, preferred_element_type=F32))
    return jnp.concatenate(outs, axis=1)


def _swap_pairs(x):
    n = x.shape[1]
    lane = lax.broadcasted_iota(jnp.int32, x.shape, 1)
    nxt = pltpu.roll(x, n - 1, 1)
    prv = pltpu.roll(x, 1, 1)
    return jnp.where((lane & 1) == 0, nxt, prv)


def _in_proj_kernel(sections, q_scale,
                    x_ref, ng_ref, sh_ref, sc_ref, w_ref, lng_ref, qg_ref, kg_ref,
                    cos_ref, sin_ref, ones_ref, o_ref, vt_ref):
    x = x_ref[...]
    y = x * lax.rsqrt(jnp.mean(x * x, axis=-1, keepdims=True) + EPS) * ng_ref[...]
    h = (y * (1.0 + sc_ref[...]) + sh_ref[...]).astype(w_ref.dtype)
    tn = w_ref.shape[1] // len(sections)

    def qk_norm_rope(z, g_ref, scale):
        ms = _group_sum_64(z * z, ones_ref[...]) * (1.0 / DIFF_QK)
        y = z * lax.rsqrt(ms + EPS) * g_ref[...]
        reps = tn // LANES
        cos = jnp.concatenate([cos_ref[...]] * reps, axis=1)
        sin = jnp.concatenate([sin_ref[...]] * reps, axis=1)
        y = y * cos + _swap_pairs(y) * sin
        if scale != 1.0:
            y = y * scale
        return y

    for idx, kind in enumerate(sections):
        acc = jnp.dot(h, w_ref[:, idx * tn:(idx + 1) * tn], preferred_element_type=F32)
        cols = slice(idx * tn, (idx + 1) * tn)
        if kind == "u":
            o_ref[:, cols] = _gelu(acc).astype(o_ref.dtype)
        elif kind == "va":
            g = _gelu(acc)
            for b in range(tn // CHUNK):
                blk = g[:, b * CHUNK:(b + 1) * CHUNK]
                mu = jnp.mean(blk, axis=-1, keepdims=True)
                d = blk - mu
                y = d * lax.rsqrt(jnp.mean(d * d, axis=-1, keepdims=True) + EPS)
                y = y * lng_ref[:, b * CHUNK:(b + 1) * CHUNK]
                o_ref[:, idx * tn + b * CHUNK:idx * tn + (b + 1) * CHUNK] = y.astype(o_ref.dtype)
        elif kind == "q":
            o_ref[:, cols] = qk_norm_rope(acc, qg_ref, q_scale).astype(o_ref.dtype)
        elif kind == "k":
            o_ref[:, cols] = qk_norm_rope(acc, kg_ref, 1.0).astype(o_ref.dtype)
        else:
            vt_ref[...] = acc.T.astype(vt_ref.dtype)


def _in_proj(x2, ng, sh, sc, w, lng, qg, kg, cos, sin, ones_bd, sections, q_scale, tm):
    l, d = x2.shape
    n = w.shape[1]
    tn = n // len(sections)
    assert sections[-1] == "v"
    cst = lambda i: (0, 0)
    return pl.pallas_call(
        functools.partial(_in_proj_kernel, sections, q_scale),
        grid=(l // tm,),
        in_specs=[pl.BlockSpec((tm, d), lambda i: (i, 0)),
                  pl.BlockSpec((1, d), cst), pl.BlockSpec((1, d), cst), pl.BlockSpec((1, d), cst),
                  pl.BlockSpec((d, n), cst, pipeline_mode=pl.Buffered(1)),
                  pl.BlockSpec((1, tn), cst), pl.BlockSpec((1, tn), cst), pl.BlockSpec((1, tn), cst),
                  pl.BlockSpec((tm, LANES), lambda i: (i, 0)),
                  pl.BlockSpec((tm, LANES), lambda i: (i, 0)),
                  pl.BlockSpec((LANES, LANES), cst)],
        out_specs=[pl.BlockSpec((tm, n - tn), lambda i: (i, 0)),
                   pl.BlockSpec((tn, tm), lambda i: (0, i))],
        out_shape=[jax.ShapeDtypeStruct((l, n - tn), MXU_DTYPE),
                   jax.ShapeDtypeStruct((tn, l), MXU_DTYPE)],
        compiler_params=_cparams(("arbitrary",)), name="in_proj",
    )(x2, ng, sh, sc, w, lng, qg, kg, cos, sin, ones_bd)


def _attn_kernel(tk, lam_init,
                 q_ref, k_ref, vt_ref, kc_ref, vct_ref, g_ref, l1_ref, l2_ref, l3_ref, l4_ref,
                 o_ref, sa_ref, sb_ref, pa_ref, pb_ref):
    tq = q_ref.shape[0]
    qt = q_ref[...].astype(F32).T.astype(MXU_DTYPE)
    sub = lax.broadcasted_iota(jnp.int32, qt.shape, 0)
    zero = jnp.zeros_like(qt)
    qbd = jnp.concatenate([jnp.where(sub < DIFF_QK, qt, zero),
                           jnp.where(sub >= DIFF_QK, qt, zero)], axis=1)
    n_chunks = k_ref.shape[0] // tk

    def scores(c, s_ref):
        s_ref[...] = jnp.dot(k_ref[c * tk:(c + 1) * tk, :], qbd, preferred_element_type=F32)

    def softmax(s_ref, p_ref, state):
        m, l, _, acc = state
        s = s_ref[...]
        m_new = jnp.maximum(m, jnp.max(s, axis=0, keepdims=True))
        p = jnp.exp2(s - m_new)
        alpha = jnp.exp2(m - m_new)
        l = alpha * l + jnp.sum(p, axis=0, keepdims=True)
        p_ref[...] = p.astype(p_ref.dtype)
        return m_new, l, alpha, acc

    def values(c, p_ref, state):
        m, l, alpha, acc = state
        vt = vt_ref[:, c * tk:(c + 1) * tk]
        return m, l, alpha, alpha * acc + jnp.dot(vt, p_ref[...], preferred_element_type=F32)

    s = jnp.dot(kc_ref[...], qbd, preferred_element_type=F32)
    m = jnp.max(s, axis=0, keepdims=True)
    p = jnp.exp2(s - m)
    state = (m, jnp.sum(p, axis=0, keepdims=True), jnp.ones_like(m),
             jnp.dot(vct_ref[...], p.astype(MXU_DTYPE), preferred_element_type=F32))

    s_bufs, p_bufs = (sa_ref, sb_ref), (pa_ref, pb_ref)
    scores(0, s_bufs[0])
    for c in range(n_chunks):
        if c + 1 < n_chunks:
            scores(c + 1, s_bufs[(c + 1) % 2])
        if c >= 1:
            state = values(c - 1, p_bufs[(c - 1) % 2], state)
        state = softmax(s_bufs[c % 2], p_bufs[c % 2], state)
    _, l, _, acc = values(n_chunks - 1, p_bufs[(n_chunks - 1) % 2], state)

    lam = (jnp.exp(jnp.sum(l1_ref[...] * l2_ref[...], axis=-1, keepdims=True))
           - jnp.exp(jnp.sum(l3_ref[...] * l4_ref[...], axis=-1, keepdims=True)) + lam_init)
    o = acc[:, :tq] / l[:, :tq] - lam * (acc[:, tq:] / l[:, tq:])
    y = o * lax.rsqrt(jnp.mean(o * o, axis=0, keepdims=True) + EPS) * g_ref[...]
    o_ref[...] = (y * (1.0 - lam_init)).T.astype(o_ref.dtype)


def _attention(z, vt, zc, vct, subln_g, lq1, lk1, lq2, lk2, lam_init, tq, tk):
    l = z.shape[0]
    lc = zc.shape[0]
    nh = DIFF_HEADS
    qb, kb = 2 * nh, 3 * nh
    vec = lambda a: a.reshape(1, -1).astype(F32)
    cst = lambda h, i: (0, 0)
    return pl.pallas_call(
        functools.partial(_attn_kernel, tk, lam_init),
        grid=(nh, l // tq),
        in_specs=[pl.BlockSpec((tq, LANES), lambda h, i: (i, qb + h)),
                  pl.BlockSpec((l, LANES), lambda h, i: (0, kb + h)),
                  pl.BlockSpec((DIFF_V, l), lambda h, i: (h, 0)),
                  pl.BlockSpec((lc, LANES), lambda h, i: (0, h)),
                  pl.BlockSpec((DIFF_V, lc), lambda h, i: (h, 0)),
                  pl.BlockSpec((DIFF_V, 1), cst),
                  pl.BlockSpec((1, DIFF_QK), cst), pl.BlockSpec((1, DIFF_QK), cst),
                  pl.BlockSpec((1, DIFF_QK), cst), pl.BlockSpec((1, DIFF_QK), cst)],
        out_specs=pl.BlockSpec((tq, DIFF_V), lambda h, i: (i, h)),
        out_shape=jax.ShapeDtypeStruct((l, nh * DIFF_V), MXU_DTYPE),
        scratch_shapes=[pltpu.VMEM((tk, 2 * tq), F32), pltpu.VMEM((tk, 2 * tq), F32),
                        pltpu.VMEM((tk, 2 * tq), MXU_DTYPE), pltpu.VMEM((tk, 2 * tq), MXU_DTYPE)],
        compiler_params=_cparams(("arbitrary", "arbitrary")), name="attention",
    )(z, z, vt, zc, vct, subln_g.reshape(-1, 1).astype(F32), vec(lq1), vec(lk1), vec(lq2), vec(lk2))


def _gmlp_kernel(u_ref, v_ref, ws_ref, bs_ref, o_ref):
    tm = u_ref.shape[0]
    for c in range(tm // CHUNK):
        rows = slice(c * CHUNK, (c + 1) * CHUNK)
        for g in range(GMLP_GROUPS):
            cols = slice(g * CHUNK, (g + 1) * CHUNK)
            mixed = jnp.dot(ws_ref[g], v_ref[rows, cols], preferred_element_type=F32)
            mixed = mixed + bs_ref[:, g:g + 1]
            o_ref[rows, cols] = (u_ref[rows, cols].astype(F32) * mixed).astype(o_ref.dtype)


def _gmlp(z, ws, bs_t, tm):
    l = z.shape[0]
    da = GMLP_GROUPS * CHUNK
    return pl.pallas_call(
        _gmlp_kernel,
        grid=(l // tm,),
        in_specs=[pl.BlockSpec((tm, da), lambda i: (i, 0)),
                  pl.BlockSpec((tm, da), lambda i: (i, 1)),
                  pl.BlockSpec((GMLP_GROUPS, CHUNK, CHUNK), lambda i: (0, 0, 0)),
                  pl.BlockSpec((CHUNK, GMLP_GROUPS), lambda i: (0, 0))],
        out_specs=pl.BlockSpec((tm, da), lambda i: (i, 0)),
        out_shape=jax.ShapeDtypeStruct((l, da), MXU_DTYPE),
        compiler_params=_cparams(("arbitrary",)), name="gmlp",
    )(z, z, ws, bs_t)


def _out_proj_kernel(a_ref, t_ref, w_ref, x_ref, g1_ref, ng_ref, sh_ref, sc_ref, xn_ref, h2_ref):
    da = a_ref.shape[1]
    mix = (jnp.dot(a_ref[...], w_ref[0:da, :], preferred_element_type=F32)
           + jnp.dot(t_ref[...], w_ref[da:, :], preferred_element_type=F32))
    xn = x_ref[...] + g1_ref[...] * mix
    xn_ref[...] = xn
    y = xn * lax.rsqrt(jnp.mean(xn * xn, axis=-1, keepdims=True) + EPS) * ng_ref[...]
    h2_ref[...] = (y * (1.0 + sc_ref[...]) + sh_ref[...]).T.astype(h2_ref.dtype)


def _out_proj(a_out, attn, w_out, x2, g1, ng, sh, sc, tm):
    l, d = x2.shape
    da, dv = a_out.shape[1], attn.shape[1]
    cst = lambda i: (0, 0)
    return pl.pallas_call(
        _out_proj_kernel,
        grid=(l // tm,),
        in_specs=[pl.BlockSpec((tm, da), lambda i: (i, 0)),
                  pl.BlockSpec((tm, dv), lambda i: (i, 0)),
                  pl.BlockSpec((da + dv, d), cst),
                  pl.BlockSpec((tm, d), lambda i: (i, 0)),
                  pl.BlockSpec((1, d), cst), pl.BlockSpec((1, d), cst),
                  pl.BlockSpec((1, d), cst), pl.BlockSpec((1, d), cst)],
        out_specs=[pl.BlockSpec((tm, d), lambda i: (i, 0)),
                   pl.BlockSpec((d, tm), lambda i: (0, i))],
        out_shape=[jax.ShapeDtypeStruct((l, d), F32), jax.ShapeDtypeStruct((d, l), MXU_DTYPE)],
        compiler_params=_cparams(("arbitrary",)), name="out_proj",
    )(a_out, attn, w_out, x2, g1, ng, sh, sc)


N_CAND = PEER_TOPK + 8 * (PEER_TOPK - 1)


def _peer_sel_kernel(ht_ref, wqt_ref, keys_ref, th_ref, ea_ref, bt_ref, eb_ref,
                     qp_ref, at_ref, as_ref, bs_ref, cand_ref):
    tm = ht_ref.shape[1]
    qp_ref[...] = jnp.dot(wqt_ref[...], ht_ref[...],
                          preferred_element_type=F32).astype(qp_ref.dtype)
    nh = th_ref.shape[0]

    def top16(s, out_ref):
        for r in range(PEER_TOPK):
            m = jnp.max(s, axis=0, keepdims=True)
            out_ref[r:r + 1, :] = m
            if r < PEER_TOPK - 1:
                s = jnp.where(s == m, NEG_BIG, s)

    def strip(h, c0):
        cols = pl.ds(c0, LANES)
        a = at_ref[:, cols]
        b = bt_ref[h, :, cols]
        top16(a, as_ref)
        top16(b, bs_ref)
        cand_ref[0:PEER_TOPK, :] = as_ref[0:1, :] + bs_ref[...]
        for i in range(1, PEER_TOPK):
            r0 = PEER_TOPK + 8 * (i - 1)
            cand_ref[r0:r0 + 8, :] = as_ref[i:i + 1, :] + bs_ref[0:8, :]
        c = cand_ref[...]
        cmax = jnp.max(c, axis=0, keepdims=True)
        m = cmax
        zsum = jnp.ones_like(cmax)
        for r in range(1, PEER_TOPK):
            c = jnp.where(c == m, NEG_BIG, c)
            m = jnp.max(c, axis=0, keepdims=True)
            zsum = zsum + jnp.exp(m - cmax)
        tau = m
        bs = bs_ref[...]
        th = jnp.full(a.shape, POS_BIG, F32)
        for r in range(PEER_TOPK):
            ar = as_ref[r:r + 1, :]
            th_r = jnp.min(jnp.where(ar + bs >= tau, bs, POS_BIG), axis=0, keepdims=True)
            th = jnp.where(a == ar, th_r, th)
        th_ref[h, :, cols] = th
        ea_ref[h, :, cols] = jnp.exp(a - as_ref[0:1, :])
        eb_ref[h, :, cols] = jnp.exp(b - bs_ref[0:1, :]) / zsum

    def head(h, carry):
        ra = pl.multiple_of(2 * h * N_KEYS, N_KEYS)
        rb = pl.multiple_of((2 * h + 1) * N_KEYS, N_KEYS)
        at_ref[...] = jnp.dot(keys_ref[2 * h], qp_ref[pl.ds(ra, N_KEYS), :],
                              preferred_element_type=F32)
        bt_ref[h] = jnp.dot(keys_ref[2 * h + 1], qp_ref[pl.ds(rb, N_KEYS), :],
                            preferred_element_type=F32)

        def body(s, carry):
            strip(h, pl.multiple_of(s * LANES, LANES))
            return carry

        return lax.fori_loop(0, tm // LANES, body, carry)

    lax.fori_loop(0, nh, head, 0)


def _peer_sel(h2t, wqt, keys, tm):
    d, l = h2t.shape
    nh = keys.shape[0] // 2
    blk = pl.BlockSpec((nh, N_KEYS, tm), lambda i: (0, 0, i))
    shp = jax.ShapeDtypeStruct((nh, N_KEYS, l), F32)
    return pl.pallas_call(
        _peer_sel_kernel,
        grid=(l // tm,),
        in_specs=[pl.BlockSpec((d, tm), lambda i: (0, i)),
                  pl.BlockSpec(wqt.shape, lambda i: (0, 0)),
                  pl.BlockSpec(keys.shape, lambda i: (0, 0, 0))],
        out_specs=[blk, blk, blk, blk],
        out_shape=[shp, shp, shp, shp],
        scratch_shapes=[pltpu.VMEM((wqt.shape[0], tm), MXU_DTYPE),
                        pltpu.VMEM((N_KEYS, tm), F32),
                        pltpu.VMEM((PEER_TOPK, LANES), F32), pltpu.VMEM((PEER_TOPK, LANES), F32),
                        pltpu.VMEM((N_CAND, LANES), F32)],
        compiler_params=_cparams(("arbitrary",)), name="peer_sel",
    )(h2t, wqt, keys)


def _peer_dense_kernel(ht_ref, u_ref, vt_ref, th_ref, ea_ref, bt_ref, eb_ref, x_ref, g2_ref, o_ref,
                       act_ref, acc_ref):
    e = pl.program_id(1)
    n_tiles = pl.num_programs(1) - 1
    nh = th_ref.shape[0]
    ni = u_ref.shape[0] // N_KEYS

    @pl.when(e == 0)
    def _():
        acc_ref[...] = jnp.zeros_like(acc_ref)
        act_ref[...] = jnp.dot(u_ref[...], ht_ref[...], preferred_element_type=F32)

    @pl.when(e > 0)
    def _():
        act = _gelu(act_ref[...])
        act_ref[...] = jnp.dot(u_ref[...], ht_ref[...], preferred_element_type=F32)
        ws = []
        for il in range(ni):
            i_glob = (e - 1) * ni + il
            g = None
            for h in range(nh):
                th = th_ref[h, pl.ds(i_glob, 1), :]
                ea = ea_ref[h, pl.ds(i_glob, 1), :]
                t = jnp.where(bt_ref[h] >= th, eb_ref[h], 0.0) * ea
                g = t if g is None else g + t
            ws.append((g * act[il * N_KEYS:(il + 1) * N_KEYS, :]).astype(vt_ref.dtype))
        w = jnp.concatenate(ws, axis=0)
        acc_ref[...] += jnp.dot(vt_ref[...], w, preferred_element_type=F32)

    @pl.when(e == n_tiles)
    def _():
        o_ref[...] = x_ref[...] + g2_ref[...] * acc_ref[...].T


def _peer_dense(h2t, u, vt, th, ea, bt, eb, xn, g2, tt, te):
    d, l = h2t.shape
    n_tiles = u.shape[0] // te
    nh = th.shape[0]
    sel = pl.BlockSpec((nh, N_KEYS, tt), lambda i, e: (0, 0, i))
    return pl.pallas_call(
        _peer_dense_kernel,
        grid=(l // tt, n_tiles + 1),
        in_specs=[pl.BlockSpec((d, tt), lambda i, e: (0, i)),
                  pl.BlockSpec((te, d), lambda i, e: (jnp.minimum(e, n_tiles - 1), 0)),
                  pl.BlockSpec((d, te), lambda i, e: (0, jnp.maximum(e - 1, 0))),
                  sel, sel, sel, sel,
                  pl.BlockSpec((tt, d), lambda i, e: (i, 0)),
                  pl.BlockSpec((1, d), lambda i, e: (0, 0))],
        out_specs=pl.BlockSpec((tt, d), lambda i, e: (i, 0)),
        out_shape=jax.ShapeDtypeStruct((l, d), F32),
        scratch_shapes=[pltpu.VMEM((te, tt), F32), pltpu.VMEM((d, tt), F32)],
        compiler_params=_cparams(("arbitrary", "arbitrary")), name="peer_dense",
    )(h2t, u, vt, th, ea, bt, eb, xn, g2)


def _rope_tables(n_tok):
    n_rows = n_tok // GRID_W
    rows = jnp.repeat(jnp.arange(n_rows, dtype=F32), GRID_W)
    cols = jnp.tile(jnp.arange(GRID_W, dtype=F32), n_rows)
    n_freq = DIFF_QK // 4
    inv = ROPE_BASE ** (-jnp.arange(n_freq, dtype=F32) / n_freq)
    ang = jnp.concatenate([rows[:, None] * inv, cols[:, None] * inv], axis=-1)
    ang = jnp.repeat(ang, 2, axis=-1)
    sign = jnp.tile(jnp.array([-1.0, 1.0], F32), DIFF_QK // 2)
    cos = jnp.tile(jnp.cos(ang), (1, LANES // DIFF_QK))
    sin = jnp.tile(jnp.sin(ang) * sign, (1, LANES // DIFF_QK))
    return cos, sin


def kernel(x, c, ctx, c_ctx, w_ada, b_ada, norm1_g, norm2_g, w_in, gmlp_ln_g, gmlp_ws, gmlp_bs,
           q_norm_g, k_norm_g, lambda_q1, lambda_k1, lambda_q2, lambda_k2, subln_g, w_out,
           peer_wq, peer_keys, peer_u, peer_v):
    depth = w_ada.shape[0]
    assert depth == 1 and x.shape[0] == 1
    _, seq, d = x.shape
    lc = ctx.shape[1]
    da = GMLP_GROUPS * CHUNK
    dqk = DIFF_HEADS * 2 * DIFF_QK
    lam_init = 0.8 - 0.6 * math.exp(-0.3 * 0)
    row = lambda a: a.reshape(1, -1).astype(F32)

    cond8 = jnp.zeros((8, d), F32).at[0].set(c[0]).at[1].set(c_ctx)
    mod = _ada(cond8, w_ada[0], b_ada[0])
    sh1, sc1, g1, sh2, sc2, g2 = [mod[0:1, k * d:(k + 1) * d] for k in range(6)]
    sh1c, sc1c = mod[1:2, 0:d], mod[1:2, d:2 * d]

    w_in_b = w_in[0].astype(MXU_DTYPE)
    ones_bd = jnp.kron(jnp.eye(LANES // DIFF_QK, dtype=F32),
                       jnp.ones((DIFF_QK, DIFF_QK), F32)).astype(MXU_DTYPE)
    qg = jnp.tile(row(q_norm_g[0]), (1, dqk // DIFF_QK))
    kg = jnp.tile(row(k_norm_g[0]), (1, dqk // DIFF_QK))
    lng = row(gmlp_ln_g[0])
    cos, sin = _rope_tables(seq)
    q_scale = (DIFF_QK ** -0.5) * math.log2(math.e)

    z, vt = _in_proj(x[0], row(norm1_g[0]), sh1, sc1, w_in_b, lng, qg, kg, cos, sin, ones_bd,
                     ("u", "va", "q", "k", "v"), q_scale, tm=min(512, seq))
    zc, vct = _in_proj(ctx[0], row(norm1_g[0]), sh1c, sc1c, w_in_b[:, 2 * da + dqk:], lng, qg, kg,
                       jnp.ones((lc, LANES), F32), jnp.zeros((lc, LANES), F32), ones_bd,
                       ("k", "v"), 1.0, tm=lc)

    attn = _attention(z, vt, zc, vct, subln_g[0], lambda_q1[0], lambda_k1[0], lambda_q2[0],
                      lambda_k2[0], lam_init, tq=min(256, seq), tk=min(2048, seq))
    a_out = _gmlp(z, gmlp_ws[0].astype(MXU_DTYPE), gmlp_bs[0].T.astype(F32), tm=min(512, seq))
    xn, h2t = _out_proj(a_out, attn, w_out[0].astype(MXU_DTYPE), x[0], g1, row(norm2_g[0]),
                        sh2, sc2, tm=min(256, seq))

    keys = peer_keys[0].reshape(PEER_HEADS * 2, N_KEYS, -1).astype(MXU_DTYPE)
    th, ea, bt, eb = _peer_sel(h2t, peer_wq[0].T.astype(MXU_DTYPE), keys, tm=min(512, seq))
    out = _peer_dense(h2t, peer_u[0].astype(MXU_DTYPE), peer_v[0].T.astype(MXU_DTYPE),
                      th, ea, bt, eb, xn, g2, tt=min(512, seq), te=512)
    return out[None]
```

```python
import functools
import math

import jax
import jax.numpy as jnp
from jax import lax
from jax.experimental import pallas as pl
from jax.experimental.pallas import tpu as pltpu

F32 = jnp.float32
MXU_DTYPE = jnp.bfloat16

EPS = 1e-6
GRID_W = 64
ROPE_BASE = 10000.0

GMLP_GROUPS = 8
CHUNK = 128
DIFF_HEADS = 8
DIFF_QK = 64
DIFF_V = 128
PEER_HEADS = 8
N_KEYS = 128
PEER_TOPK = 16

LANES = 128
NEG_BIG = -1e30
POS_BIG = 1e30

VMEM_LIMIT = 56 * 1024 * 1024


def _cparams(sem):
    return pltpu.CompilerParams(dimension_semantics=sem, vmem_limit_bytes=VMEM_LIMIT)


def _gelu(x):
    return 0.5 * x * (1.0 + lax.erf(x * (2.0 ** -0.5)))


def _ada_kernel(c_ref, w_ref, b_ref, o_ref):
    c = c_ref[...]
    s = c * jax.nn.sigmoid(c)
    o_ref[...] = jnp.dot(s.astype(MXU_DTYPE), w_ref[...].astype(MXU_DTYPE),
                         preferred_element_type=F32) + b_ref[...]


def _ada(cond8, w_ada, b_ada):
    d, n = w_ada.shape
    tn = 1024
    return pl.pallas_call(
        _ada_kernel,
        grid=(n // tn,),
        in_specs=[pl.BlockSpec((8, d), lambda j: (0, 0)),
                  pl.BlockSpec((d, tn), lambda j: (0, j)),
                  pl.BlockSpec((1, tn), lambda j: (0, j))],
        out_specs=pl.BlockSpec((8, tn), lambda j: (0, j)),
        out_shape=jax.ShapeDtypeStruct((8, n), F32),
        compiler_params=_cparams(("arbitrary",)), name="ada",
    )(cond8, w_ada, b_ada.reshape(1, n))


def _group_sum_64(xx, ones_bd):
    outs = []
    for b in range(xx.shape[1] // LANES):
        blk = xx[:, b * LANES:(b + 1) * LANES]
        hi = blk.astype(MXU_DTYPE)
        lo = (blk - hi.astype(F32)).astype(MXU_DTYPE)
        outs.append(jnp.dot(hi, ones_bd, preferred_element_type=F32)
                    + jnp.dot(lo, ones_bd, preferred_element_type=F32))
    return jnp.concatenate(outs, axis=1)


def _swap_pairs(x):
    n = x.shape[1]
    lane = lax.broadcasted_iota(jnp.int32, x.shape, 1)
    nxt = pltpu.roll(x, n - 1, 1)
    prv = pltpu.roll(x, 1, 1)
    return jnp.where((lane & 1) == 0, nxt, prv)


def _in_proj_kernel(sections, q_scale,
                    x_ref, ng_ref, sh_ref, sc_ref, w_ref, lng_ref, qg_ref, kg_ref,
                    cos_ref, sin_ref, ones_ref, o_ref, vt_ref):
    x = x_ref[...]
    y = x * lax.rsqrt(jnp.mean(x * x, axis=-1, keepdims=True) + EPS) * ng_ref[...]
    h = (y * (1.0 + sc_ref[...]) + sh_ref[...]).astype(w_ref.dtype)
    tn = w_ref.shape[1] // len(sections)

    def qk_norm_rope(z, g_ref, scale):
        ms = _group_sum_64(z * z, ones_ref[...]) * (1.0 / DIFF_QK)
        y = z * lax.rsqrt(ms + EPS) * g_ref[...]
        reps = tn // LANES
        cos = jnp.concatenate([cos_ref[...]] * reps, axis=1)
        sin = jnp.concatenate([sin_ref[...]] * reps, axis=1)
        y = y * cos + _swap_pairs(y) * sin
        if scale != 1.0:
            y = y * scale
        return y

    for idx, kind in enumerate(sections):
        acc = jnp.dot(h, w_ref[:, idx * tn:(idx + 1) * tn], preferred_element_type=F32)
        cols = slice(idx * tn, (idx + 1) * tn)
        if kind == "u":
            o_ref[:, cols] = _gelu(acc).astype(o_ref.dtype)
        elif kind == "va":
            g = _gelu(acc)
            for b in range(tn // CHUNK):
                blk = g[:, b * CHUNK:(b + 1) * CHUNK]
                mu = jnp.mean(blk, axis=-1, keepdims=True)
                d = blk - mu
                y = d * lax.rsqrt(jnp.mean(d * d, axis=-1, keepdims=True) + EPS)
                y = y * lng_ref[:, b * CHUNK:(b + 1) * CHUNK]
                o_ref[:, idx * tn + b * CHUNK:idx * tn + (b + 1) * CHUNK] = y.astype(o_ref.dtype)
        elif kind == "q":
            o_ref[:, cols] = qk_norm_rope(acc, qg_ref, q_scale).astype(o_ref.dtype)
        elif kind == "k":
            o_ref[:, cols] = qk_norm_rope(acc, kg_ref, 1.0).astype(o_ref.dtype)
        else:
            vt_ref[...] = acc.T.astype(vt_ref.dtype)


def _in_proj(x2, ng, sh, sc, w, lng, qg, kg, cos, sin, ones_bd, sections, q_scale, tm):
    l, d = x2.shape
    n = w.shape[1]
    tn = n // len(sections)
    assert sections[-1] == "v"
    cst = lambda i: (0, 0)
    return pl.pallas_call(
        functools.partial(_in_proj_kernel, sections, q_scale),
        grid=(l // tm,),
        in_specs=[pl.BlockSpec((tm, d), lambda i: (i, 0)),
                  pl.BlockSpec((1, d), cst), pl.BlockSpec((1, d), cst), pl.BlockSpec((1, d), cst),
                  pl.BlockSpec((d, n), cst, pipeline_mode=pl.Buffered(1)),
                  pl.BlockSpec((1, tn), cst), pl.BlockSpec((1, tn), cst), pl.BlockSpec((1, tn), cst),
                  pl.BlockSpec((tm, LANES), lambda i: (i, 0)),
                  pl.BlockSpec((tm, LANES), lambda i: (i, 0)),
                  pl.BlockSpec((LANES, LANES), cst)],
        out_specs=[pl.BlockSpec((tm, n - tn), lambda i: (i, 0)),
                   pl.BlockSpec((tn, tm), lambda i: (0, i))],
        out_shape=[jax.ShapeDtypeStruct((l, n - tn), MXU_DTYPE),
                   jax.ShapeDtypeStruct((tn, l), MXU_DTYPE)],
        compiler_params=_cparams(("arbitrary",)), name="in_proj",
    )(x2, ng, sh, sc, w, lng, qg, kg, cos, sin, ones_bd)


def _attn_kernel(tk, lam_init,
                 q_ref, k_ref, vt_ref, kc_ref, vct_ref, g_ref, l1_ref, l2_ref, l3_ref, l4_ref,
                 o_ref, sa_ref, sb_ref, pa_ref, pb_ref):
    tq = q_ref.shape[0]
    qt = q_ref[...].astype(F32).T.astype(MXU_DTYPE)
    sub = lax.broadcasted_iota(jnp.int32, qt.shape, 0)
    zero = jnp.zeros_like(qt)
    qbd = jnp.concatenate([jnp.where(sub < DIFF_QK, qt, zero),
                           jnp.where(sub >= DIFF_QK, qt, zero)], axis=1)
    n_chunks = k_ref.shape[0] // tk

    def scores(c, s_ref):
        s_ref[...] = jnp.dot(k_ref[c * tk:(c + 1) * tk, :], qbd, preferred_element_type=F32)

    def softmax(s_ref, p_ref, state):
        m, l, _, acc = state
        s = s_ref[...]
        m_new = jnp.maximum(m, jnp.max(s, axis=0, keepdims=True))
        p = jnp.exp2(s - m_new)
        alpha = jnp.exp2(m - m_new)
        l = alpha * l + jnp.sum(p, axis=0, keepdims=True)
        p_ref[...] = p.astype(p_ref.dtype)
        return m_new, l, alpha, acc

    def values(c, p_ref, state):
        m, l, alpha, acc = state
        vt = vt_ref[:, c * tk:(c + 1) * tk]
        return m, l, alpha, alpha * acc + jnp.dot(vt, p_ref[...], preferred_element_type=F32)

    s = jnp.dot(kc_ref[...], qbd, preferred_element_type=F32)
    m = jnp.max(s, axis=0, keepdims=True)
    p = jnp.exp2(s - m)
    state = (m, jnp.sum(p, axis=0, keepdims=True), jnp.ones_like(m),
             jnp.dot(vct_ref[...], p.astype(MXU_DTYPE), preferred_element_type=F32))

    s_bufs, p_bufs = (sa_ref, sb_ref), (pa_ref, pb_ref)
    scores(0, s_bufs[0])
    for c in range(n_chunks):
        if c + 1 < n_chunks:
            scores(c + 1, s_bufs[(c + 1) % 2])
        if c >= 1:
            state = values(c - 1, p_bufs[(c - 1) % 2], state)
        state = softmax(s_bufs[c % 2], p_bufs[c % 2], state)
    _, l, _, acc = values(n_chunks - 1, p_bufs[(n_chunks - 1) % 2], state)

    lam = (jnp.exp(jnp.sum(l1_ref[...] * l2_ref[...], axis=-1, keepdims=True))
           - jnp.exp(jnp.sum(l3_ref[...] * l4_ref[...], axis=-1, keepdims=True)) + lam_init)
    o = acc[:, :tq] / l[:, :tq] - lam * (acc[:, tq:] / l[:, tq:])
    y = o * lax.rsqrt(jnp.mean(o * o, axis=0, keepdims=True) + EPS) * g_ref[...]
    o_ref[...] = (y * (1.0 - lam_init)).T.astype(o_ref.dtype)


def _attention(z, vt, zc, vct, subln_g, lq1, lk1, lq2, lk2, lam_init, tq, tk):
    l = z.shape[0]
    lc = zc.shape[0]
    nh = DIFF_HEADS
    qb, kb = 2 * nh, 3 * nh
    vec = lambda a: a.reshape(1, -1).astype(F32)
    cst = lambda h, i: (0, 0)
    return pl.pallas_call(
        functools.partial(_attn_kernel, tk, lam_init),
        grid=(nh, l // tq),
        in_specs=[pl.BlockSpec((tq, LANES), lambda h, i: (i, qb + h)),
                  pl.BlockSpec((l, LANES), lambda h, i: (0, kb + h)),
                  pl.BlockSpec((DIFF_V, l), lambda h, i: (h, 0)),
                  pl.BlockSpec((lc, LANES), lambda h, i: (0, h)),
                  pl.BlockSpec((DIFF_V, lc), lambda h, i: (h, 0)),
                  pl.BlockSpec((DIFF_V, 1), cst),
                  pl.BlockSpec((1, DIFF_QK), cst), pl.BlockSpec((1, DIFF_QK), cst),
                  pl.BlockSpec((1, DIFF_QK), cst), pl.BlockSpec((1, DIFF_QK), cst)],
        out_specs=pl.BlockSpec((tq, DIFF_V), lambda h, i: (i, h)),
        out_shape=jax.ShapeDtypeStruct((l, nh * DIFF_V), MXU_DTYPE),
        scratch_shapes=[pltpu.VMEM((tk, 2 * tq), F32), pltpu.VMEM((tk, 2 * tq), F32),
                        pltpu.VMEM((tk, 2 * tq), MXU_DTYPE), pltpu.VMEM((tk, 2 * tq), MXU_DTYPE)],
        compiler_params=_cparams(("arbitrary", "arbitrary")), name="attention",
    )(z, z, vt, zc, vct, subln_g.reshape(-1, 1).astype(F32), vec(lq1), vec(lk1), vec(lq2), vec(lk2))


def _gmlp_kernel(u_ref, v_ref, ws_ref, bs_ref, o_ref):
    tm = u_ref.shape[0]
    for c in range(tm // CHUNK):
        rows = slice(c * CHUNK, (c + 1) * CHUNK)
        for g in range(GMLP_GROUPS):
            cols = slice(g * CHUNK, (g + 1) * CHUNK)
            mixed = jnp.dot(ws_ref[g], v_ref[rows, cols], preferred_element_type=F32)
            mixed = mixed + bs_ref[:, g:g + 1]
            o_ref[rows, cols] = (u_ref[rows, cols].astype(F32) * mixed).astype(o_ref.dtype)


def _gmlp(z, ws, bs_t, tm):
    l = z.shape[0]
    da = GMLP_GROUPS * CHUNK
    return pl.pallas_call(
        _gmlp_kernel,
        grid=(l // tm,),
        in_specs=[pl.BlockSpec((tm, da), lambda i: (i, 0)),
                  pl.BlockSpec((tm, da), lambda i: (i, 1)),
                  pl.BlockSpec((GMLP_GROUPS, CHUNK, CHUNK), lambda i: (0, 0, 0)),
                  pl.BlockSpec((CHUNK, GMLP_GROUPS), lambda i: (0, 0))],
        out_specs=pl.BlockSpec((tm, da), lambda i: (i, 0)),
        out_shape=jax.ShapeDtypeStruct((l, da), MXU_DTYPE),
        compiler_params=_cparams(("arbitrary",)), name="gmlp",
    )(z, z, ws, bs_t)


def _out_proj_kernel(a_ref, t_ref, w_ref, x_ref, g1_ref, ng_ref, sh_ref, sc_ref, xn_ref, h2_ref):
    da = a_ref.shape[1]
    mix = (jnp.dot(a_ref[...], w_ref[0:da, :], preferred_element_type=F32)
           + jnp.dot(t_ref[...], w_ref[da:, :], preferred_element_type=F32))
    xn = x_ref[...] + g1_ref[...] * mix
    xn_ref[...] = xn
    y = xn * lax.rsqrt(jnp.mean(xn * xn, axis=-1, keepdims=True) + EPS) * ng_ref[...]
    h2_ref[...] = (y * (1.0 + sc_ref[...]) + sh_ref[...]).T.astype(h2_ref.dtype)


def _out_proj(a_out, attn, w_out, x2, g1, ng, sh, sc, tm):
    l, d = x2.shape
    da, dv = a_out.shape[1], attn.shape[1]
    cst = lambda i: (0, 0)
    return pl.pallas_call(
        _out_proj_kernel,
        grid=(l // tm,),
        in_specs=[pl.BlockSpec((tm, da), lambda i: (i, 0)),
                  pl.BlockSpec((tm, dv), lambda i: (i, 0)),
                  pl.BlockSpec((da + dv, d), cst),
                  pl.BlockSpec((tm, d), lambda i: (i, 0)),
                  pl.BlockSpec((1, d), cst), pl.BlockSpec((1, d), cst),
                  pl.BlockSpec((1, d), cst), pl.BlockSpec((1, d), cst)],
        out_specs=[pl.BlockSpec((tm, d), lambda i: (i, 0)),
                   pl.BlockSpec((d, tm), lambda i: (0, i))],
        out_shape=[jax.ShapeDtypeStruct((l, d), F32), jax.ShapeDtypeStruct((d, l), MXU_DTYPE)],
        compiler_params=_cparams(("arbitrary",)), name="out_proj",
    )(a_out, attn, w_out, x2, g1, ng, sh, sc)


SUBL = 8
N_CAND = 10 * SUBL

_SORT16 = ((0, 1), (2, 3), (0, 2), (1, 3), (1, 2), (4, 5), (6, 7), (4, 6), (5, 7), (5, 6), (0, 4),
           (2, 6), (2, 4), (1, 5), (3, 7), (3, 5), (1, 2), (3, 4), (5, 6), (8, 9), (10, 11), (8, 10),
           (9, 11), (9, 10), (12, 13), (14, 15), (12, 14), (13, 15), (13, 14), (8, 12), (10, 14),
           (10, 12), (9, 13), (11, 15), (11, 13), (9, 10), (11, 12), (13, 14), (0, 8), (4, 12),
           (4, 8), (2, 10), (6, 14), (6, 10), (2, 4), (6, 8), (10, 12), (1, 9), (5, 13), (5, 9),
           (3, 11), (7, 15), (7, 11), (3, 5), (7, 9), (11, 13), (1, 2), (3, 4), (5, 6), (7, 8),
           (9, 10), (11, 12), (13, 14))


def _peer_sel_kernel(ht_ref, wqt_ref, keys_ref, cnt_ref, ea_ref, rk_ref, eb_ref,
                     qp_ref, at_ref, bt_ref, as_ref, bs_ref, cand_ref):
    tm = ht_ref.shape[1]
    qp_ref[...] = jnp.dot(wqt_ref[...], ht_ref[...],
                          preferred_element_type=F32).astype(qp_ref.dtype)
    nh = cnt_ref.shape[0]

    def top16(s, out_ref):
        v = [s[k * SUBL:(k + 1) * SUBL, :] for k in range(N_KEYS // SUBL)]
        for i, j in _SORT16:
            v[i], v[j] = jnp.maximum(v[i], v[j]), jnp.minimum(v[i], v[j])
        for r in range(PEER_TOPK):
            m = jnp.max(v[0], axis=0, keepdims=True)
            out_ref[r:r + 1, :] = m
            hit = v[0] == m
            for k in range(PEER_TOPK - 1 - r):
                v[k] = jnp.where(hit, v[k + 1], v[k])

    def strip(h, c0):
        cols = pl.ds(c0, LANES)
        a = at_ref[:, cols]
        b = bt_ref[:, cols]
        top16(a, as_ref)
        top16(b, bs_ref)
        cand_ref[0:PEER_TOPK, :] = as_ref[0:1, :] + bs_ref[...]
        for i in range(1, SUBL):
            r0 = PEER_TOPK + SUBL * (i - 1)
            cand_ref[r0:r0 + SUBL, :] = as_ref[i:i + 1, :] + bs_ref[0:SUBL, :]
        cand_ref[N_CAND - SUBL:N_CAND, :] = as_ref[SUBL:PEER_TOPK, :] + bs_ref[0:1, :]
        c = cand_ref[...]
        cmax = jnp.max(c, axis=0, keepdims=True)
        m = cmax
        zsum = jnp.ones_like(cmax)
        for r in range(1, PEER_TOPK):
            c = jnp.where(c == m, NEG_BIG, c)
            m = jnp.max(c, axis=0, keepdims=True)
            zsum = zsum + jnp.exp(m - cmax)
        tau = m
        bs = bs_ref[...]
        cnt = jnp.zeros(a.shape, F32)
        rank = jnp.full(b.shape, float(PEER_TOPK), F32)
        for r in range(PEER_TOPK):
            ar = as_ref[r:r + 1, :]
            n_r = jnp.sum(jnp.where(ar + bs >= tau, 1.0, 0.0), axis=0, keepdims=True)
            cnt = jnp.where(a == ar, n_r, cnt)
            rank = jnp.where(b == bs_ref[r:r + 1, :], float(r), rank)
        cnt_ref[h, :, cols] = cnt
        ea_ref[h, :, cols] = jnp.exp(a - as_ref[0:1, :])
        rk_ref[h, :, cols] = rank.astype(rk_ref.dtype)
        eb_ref[h, :, cols] = (jnp.exp(b - bs_ref[0:1, :]) / zsum).astype(eb_ref.dtype)

    def head(h, carry):
        ra = pl.multiple_of(2 * h * N_KEYS, N_KEYS)
        rb = pl.multiple_of((2 * h + 1) * N_KEYS, N_KEYS)
        at_ref[...] = jnp.dot(keys_ref[2 * h], qp_ref[pl.ds(ra, N_KEYS), :],
                              preferred_element_type=F32)
        bt_ref[...] = jnp.dot(keys_ref[2 * h + 1], qp_ref[pl.ds(rb, N_KEYS), :],
                              preferred_element_type=F32)

        def body(s, carry):
            strip(h, pl.multiple_of(s * LANES, LANES))
            return carry

        return lax.fori_loop(0, tm // LANES, body, carry)

    lax.fori_loop(0, nh, head, 0)


def _peer_sel(h2t, wqt, keys, tm):
    d, l = h2t.shape
    nh = keys.shape[0] // 2
    blk = pl.BlockSpec((nh, N_KEYS, tm), lambda i: (0, 0, i))
    by_a = jax.ShapeDtypeStruct((nh, N_KEYS, l), F32)
    by_b = jax.ShapeDtypeStruct((nh, N_KEYS, l), MXU_DTYPE)
    return pl.pallas_call(
        _peer_sel_kernel,
        grid=(l // tm,),
        in_specs=[pl.BlockSpec((d, tm), lambda i: (0, i)),
                  pl.BlockSpec(wqt.shape, lambda i: (0, 0)),
                  pl.BlockSpec(keys.shape, lambda i: (0, 0, 0))],
        out_specs=[blk, blk, blk, blk],
        out_shape=[by_a, by_a, by_b, by_b],
        scratch_shapes=[pltpu.VMEM((wqt.shape[0], tm), MXU_DTYPE),
                        pltpu.VMEM((N_KEYS, tm), F32), pltpu.VMEM((N_KEYS, tm), F32),
                        pltpu.VMEM((PEER_TOPK, LANES), F32), pltpu.VMEM((PEER_TOPK, LANES), F32),
                        pltpu.VMEM((N_CAND, LANES), F32)],
        compiler_params=_cparams(("arbitrary",)), name="peer_sel",
    )(h2t, wqt, keys)


def _peer_dense_kernel(ht_ref, u_ref, vt_ref, cnt_ref, ea_ref, rk_ref, eb_ref, x_ref, g2_ref, o_ref,
                       act_ref, acc_ref):
    e = pl.program_id(1)
    n_tiles = pl.num_programs(1) - 1
    nh = cnt_ref.shape[0]
    ni = u_ref.shape[0] // N_KEYS
    gdt = rk_ref.dtype
    tile_rows = 16

    def row_bcast(row):
        one = jnp.broadcast_to(row, (tile_rows, row.shape[1])).astype(gdt)
        return jnp.concatenate([one] * (N_KEYS // tile_rows), axis=0)

    @pl.when(e == 0)
    def _():
        acc_ref[...] = jnp.zeros_like(acc_ref)
        act_ref[...] = jnp.dot(u_ref[...], ht_ref[...], preferred_element_type=F32)

    @pl.when(e > 0)
    def _():
        act = _gelu(act_ref[...]).astype(gdt)
        act_ref[...] = jnp.dot(u_ref[...], ht_ref[...], preferred_element_type=F32)
        ws = []
        for il in range(ni):
            i_glob = (e - 1) * ni + il
            g = None
            for h in range(nh):
                cnt = row_bcast(cnt_ref[h, pl.ds(i_glob, 1), :])
                ea = row_bcast(ea_ref[h, pl.ds(i_glob, 1), :])
                t = jnp.where(rk_ref[h] < cnt, eb_ref[h], jnp.zeros((), gdt)) * ea
                g = t if g is None else g + t
            ws.append(g * act[il * N_KEYS:(il + 1) * N_KEYS, :])
        w = jnp.concatenate(ws, axis=0)
        acc_ref[...] += jnp.dot(vt_ref[...], w, preferred_element_type=F32)

    @pl.when(e == n_tiles)
    def _():
        o_ref[...] = x_ref[...] + g2_ref[...] * acc_ref[...].T


def _peer_dense(h2t, u, vt, cnt, ea, rk, eb, xn, g2, tt, te):
    d, l = h2t.shape
    n_tiles = u.shape[0] // te
    nh = cnt.shape[0]
    sel = pl.BlockSpec((nh, N_KEYS, tt), lambda i, e: (0, 0, i))
    return pl.pallas_call(
        _peer_dense_kernel,
        grid=(l // tt, n_tiles + 1),
        in_specs=[pl.BlockSpec((d, tt), lambda i, e: (0, i)),
                  pl.BlockSpec((te, d), lambda i, e: (jnp.minimum(e, n_tiles - 1), 0)),
                  pl.BlockSpec((d, te), lambda i, e: (0, jnp.maximum(e - 1, 0))),
                  sel, sel, sel, sel,
                  pl.BlockSpec((tt, d), lambda i, e: (i, 0)),
                  pl.BlockSpec((1, d), lambda i, e: (0, 0))],
        out_specs=pl.BlockSpec((tt, d), lambda i, e: (i, 0)),
        out_shape=jax.ShapeDtypeStruct((l, d), F32),
        scratch_shapes=[pltpu.VMEM((te, tt), F32), pltpu.VMEM((d, tt), F32)],
        compiler_params=_cparams(("arbitrary", "arbitrary")), name="peer_dense",
    )(h2t, u, vt, cnt, ea, rk, eb, xn, g2)


def _rope_tables(n_tok):
    n_rows = n_tok // GRID_W
    rows = jnp.repeat(jnp.arange(n_rows, dtype=F32), GRID_W)
    cols = jnp.tile(jnp.arange(GRID_W, dtype=F32), n_rows)
    n_freq = DIFF_QK // 4
    inv = ROPE_BASE ** (-jnp.arange(n_freq, dtype=F32) / n_freq)
    ang = jnp.concatenate([rows[:, None] * inv, cols[:, None] * inv], axis=-1)
    ang = jnp.repeat(ang, 2, axis=-1)
    sign = jnp.tile(jnp.array([-1.0, 1.0], F32), DIFF_QK // 2)
    cos = jnp.tile(jnp.cos(ang), (1, LANES // DIFF_QK))
    sin = jnp.tile(jnp.sin(ang) * sign, (1, LANES // DIFF_QK))
    return cos, sin


def kernel(x, c, ctx, c_ctx, w_ada, b_ada, norm1_g, norm2_g, w_in, gmlp_ln_g, gmlp_ws, gmlp_bs,
           q_norm_g, k_norm_g, lambda_q1, lambda_k1, lambda_q2, lambda_k2, subln_g, w_out,
           peer_wq, peer_keys, peer_u, peer_v):
    depth = w_ada.shape[0]
    assert depth == 1 and x.shape[0] == 1
    _, seq, d = x.shape
    lc = ctx.shape[1]
    da = GMLP_GROUPS * CHUNK
    dqk = DIFF_HEADS * 2 * DIFF_QK
    lam_init = 0.8 - 0.6 * math.exp(-0.3 * 0)
    row = lambda a: a.reshape(1, -1).astype(F32)

    cond8 = jnp.zeros((8, d), F32).at[0].set(c[0]).at[1].set(c_ctx)
    mod = _ada(cond8, w_ada[0], b_ada[0])
    sh1, sc1, g1, sh2, sc2, g2 = [mod[0:1, k * d:(k + 1) * d] for k in range(6)]
    sh1c, sc1c = mod[1:2, 0:d], mod[1:2, d:2 * d]

    w_in_b = w_in[0].astype(MXU_DTYPE)
    ones_bd = jnp.kron(jnp.eye(LANES // DIFF_QK, dtype=F32),
                       jnp.ones((DIFF_QK, DIFF_QK), F32)).astype(MXU_DTYPE)
    qg = jnp.tile(row(q_norm_g[0]), (1, dqk // DIFF_QK))
    kg = jnp.tile(row(k_norm_g[0]), (1, dqk // DIFF_QK))
    lng = row(gmlp_ln_g[0])
    cos, sin = _rope_tables(seq)
    q_scale = (DIFF_QK ** -0.5) * math.log2(math.e)

    z, vt = _in_proj(x[0], row(norm1_g[0]), sh1, sc1, w_in_b, lng, qg, kg, cos, sin, ones_bd,
                     ("u", "va", "q", "k", "v"), q_scale, tm=min(512, seq))
    zc, vct = _in_proj(ctx[0], row(norm1_g[0]), sh1c, sc1c, w_in_b[:, 2 * da + dqk:], lng, qg, kg,
                       jnp.ones((lc, LANES), F32), jnp.zeros((lc, LANES), F32), ones_bd,
                       ("k", "v"), 1.0, tm=lc)

    attn = _attention(z, vt, zc, vct, subln_g[0], lambda_q1[0], lambda_k1[0], lambda_q2[0],
                      lambda_k2[0], lam_init, tq=min(256, seq), tk=min(2048, seq))
    a_out = _gmlp(z, gmlp_ws[0].astype(MXU_DTYPE), gmlp_bs[0].T.astype(F32), tm=min(512, seq))
    xn, h2t = _out_proj(a_out, attn, w_out[0].astype(MXU_DTYPE), x[0], g1, row(norm2_g[0]),
                        sh2, sc2, tm=min(256, seq))

    keys = peer_keys[0].reshape(PEER_HEADS * 2, N_KEYS, -1).astype(MXU_DTYPE)
    cnt, ea, rk, eb = _peer_sel(h2t, peer_wq[0].T.astype(MXU_DTYPE), keys, tm=min(512, seq))
    out = _peer_dense(h2t, peer_u[0].astype(MXU_DTYPE), peer_v[0].T.astype(MXU_DTYPE),
                      cnt, ea, rk, eb, xn, g2, tt=min(512, seq), te=512)
    return out[None]
```

```python
import functools
import math

import jax
import jax.numpy as jnp
from jax import lax
from jax.experimental import pallas as pl
from jax.experimental.pallas import tpu as pltpu

F32 = jnp.float32
MXU_DTYPE = jnp.bfloat16

EPS = 1e-6
GRID_W = 64
ROPE_BASE = 10000.0

GMLP_GROUPS = 8
CHUNK = 128
DIFF_HEADS = 8
DIFF_QK = 64
DIFF_V = 128
PEER_HEADS = 8
N_KEYS = 128
PEER_TOPK = 16

LANES = 128
NEG_BIG = -1e30
POS_BIG = 1e30

VMEM_LIMIT = 56 * 1024 * 1024


def _cparams(sem):
    return pltpu.CompilerParams(dimension_semantics=sem, vmem_limit_bytes=VMEM_LIMIT)


def _gelu(x):
    return 0.5 * x * (1.0 + lax.erf(x * (2.0 ** -0.5)))


def _ada_kernel(c_ref, w_ref, b_ref, o_ref):
    c = c_ref[...]
    s = c * jax.nn.sigmoid(c)
    o_ref[...] = jnp.dot(s.astype(MXU_DTYPE), w_ref[...].astype(MXU_DTYPE),
                         preferred_element_type=F32) + b_ref[...]


def _ada(cond8, w_ada, b_ada):
    d, n = w_ada.shape
    tn = 1024
    return pl.pallas_call(
        _ada_kernel,
        grid=(n // tn,),
        in_specs=[pl.BlockSpec((8, d), lambda j: (0, 0)),
                  pl.BlockSpec((d, tn), lambda j: (0, j)),
                  pl.BlockSpec((1, tn), lambda j: (0, j))],
        out_specs=pl.BlockSpec((8, tn), lambda j: (0, j)),
        out_shape=jax.ShapeDtypeStruct((8, n), F32),
        compiler_params=_cparams(("arbitrary",)), name="ada",
    )(cond8, w_ada, b_ada.reshape(1, n))


def _group_sum_64(xx, ones_bd):
    outs = []
    for b in range(xx.shape[1] // LANES):
        blk = xx[:, b * LANES:(b + 1) * LANES]
        hi = blk.astype(MXU_DTYPE)
        lo = (blk - hi.astype(F32)).astype(MXU_DTYPE)
        outs.append(jnp.dot(hi, ones_bd, preferred_element_type=F32)
                    + jnp.dot(lo, ones_bd, preferred_element_type=F32))
    return jnp.concatenate(outs, axis=1)


def _swap_pairs(x):
    n = x.shape[1]
    lane = lax.broadcasted_iota(jnp.int32, x.shape, 1)
    nxt = pltpu.roll(x, n - 1, 1)
    prv = pltpu.roll(x, 1, 1)
    return jnp.where((lane & 1) == 0, nxt, prv)


def _in_proj_kernel(sections, q_scale,
                    x_ref, ng_ref, sh_ref, sc_ref, w_ref, lng_ref, qg_ref, kg_ref,
                    cos_ref, sin_ref, ones_ref, o_ref, vt_ref):
    x = x_ref[...]
    y = x * lax.rsqrt(jnp.mean(x * x, axis=-1, keepdims=True) + EPS) * ng_ref[...]
    h = (y * (1.0 + sc_ref[...]) + sh_ref[...]).astype(w_ref.dtype)
    tn = w_ref.shape[1] // len(sections)

    def qk_norm_rope(z, g_ref, scale):
        ms = _group_sum_64(z * z, ones_ref[...]) * (1.0 / DIFF_QK)
        y = z * lax.rsqrt(ms + EPS) * g_ref[...]
        reps = tn // LANES
        cos = jnp.concatenate([cos_ref[...]] * reps, axis=1)
        sin = jnp.concatenate([sin_ref[...]] * reps, axis=1)
        y = y * cos + _swap_pairs(y) * sin
        if scale != 1.0:
            y = y * scale
        return y

    for idx, kind in enumerate(sections):
        acc = jnp.dot(h, w_ref[:, idx * tn:(idx + 1) * tn], preferred_element_type=F32)
        cols = slice(idx * tn, (idx + 1) * tn)
        if kind == "u":
            o_ref[:, cols] = _gelu(acc).astype(o_ref.dtype)
        elif kind == "va":
            g = _gelu(acc)
            for b in range(tn // CHUNK):
                blk = g[:, b * CHUNK:(b + 1) * CHUNK]
                mu = jnp.mean(blk, axis=-1, keepdims=True)
                d = blk - mu
                y = d * lax.rsqrt(jnp.mean(d * d, axis=-1, keepdims=True) + EPS)
                y = y * lng_ref[:, b * CHUNK:(b + 1) * CHUNK]
                o_ref[:, idx * tn + b * CHUNK:idx * tn + (b + 1) * CHUNK] = y.astype(o_ref.dtype)
        elif kind == "q":
            o_ref[:, cols] = qk_norm_rope(acc, qg_ref, q_scale).astype(o_ref.dtype)
        elif kind == "k":
            o_ref[:, cols] = qk_norm_rope(acc, kg_ref, 1.0).astype(o_ref.dtype)
        else:
            vt_ref[...] = acc.T.astype(vt_ref.dtype)


def _in_proj(x2, ng, sh, sc, w, lng, qg, kg, cos, sin, ones_bd, sections, q_scale, tm):
    l, d = x2.shape
    n = w.shape[1]
    tn = n // len(sections)
    assert sections[-1] == "v"
    cst = lambda i: (0, 0)
    return pl.pallas_call(
        functools.partial(_in_proj_kernel, sections, q_scale),
        grid=(l // tm,),
        in_specs=[pl.BlockSpec((tm, d), lambda i: (i, 0)),
                  pl.BlockSpec((1, d), cst), pl.BlockSpec((1, d), cst), pl.BlockSpec((1, d), cst),
                  pl.BlockSpec((d, n), cst, pipeline_mode=pl.Buffered(1)),
                  pl.BlockSpec((1, tn), cst), pl.BlockSpec((1, tn), cst), pl.BlockSpec((1, tn), cst),
                  pl.BlockSpec((tm, LANES), lambda i: (i, 0)),
                  pl.BlockSpec((tm, LANES), lambda i: (i, 0)),
                  pl.BlockSpec((LANES, LANES), cst)],
        out_specs=[pl.BlockSpec((tm, n - tn), lambda i: (i, 0)),
                   pl.BlockSpec((tn, tm), lambda i: (0, i))],
        out_shape=[jax.ShapeDtypeStruct((l, n - tn), MXU_DTYPE),
                   jax.ShapeDtypeStruct((tn, l), MXU_DTYPE)],
        compiler_params=_cparams(("arbitrary",)), name="in_proj",
    )(x2, ng, sh, sc, w, lng, qg, kg, cos, sin, ones_bd)


def _attn_kernel(tk, lam_init,
                 q_ref, k_ref, vt_ref, kc_ref, vct_ref, g_ref, l1_ref, l2_ref, l3_ref, l4_ref,
                 o_ref, sa_ref, sb_ref, pa_ref, pb_ref):
    tq = q_ref.shape[0]
    qt = q_ref[...].astype(F32).T.astype(MXU_DTYPE)
    sub = lax.broadcasted_iota(jnp.int32, qt.shape, 0)
    zero = jnp.zeros_like(qt)
    qbd = jnp.concatenate([jnp.where(sub < DIFF_QK, qt, zero),
                           jnp.where(sub >= DIFF_QK, qt, zero)], axis=1)
    n_chunks = k_ref.shape[0] // tk

    def scores(c, s_ref):
        s_ref[...] = jnp.dot(k_ref[c * tk:(c + 1) * tk, :], qbd, preferred_element_type=F32)

    def softmax(s_ref, p_ref, state):
        m, l, _, acc = state
        s = s_ref[...]
        m_new = jnp.maximum(m, jnp.max(s, axis=0, keepdims=True))
        p = jnp.exp2(s - m_new)
        alpha = jnp.exp2(m - m_new)
        l = alpha * l + jnp.sum(p, axis=0, keepdims=True)
        p_ref[...] = p.astype(p_ref.dtype)
        return m_new, l, alpha, acc

    def values(c, p_ref, state):
        m, l, alpha, acc = state
        vt = vt_ref[:, c * tk:(c + 1) * tk]
        return m, l, alpha, alpha * acc + jnp.dot(vt, p_ref[...], preferred_element_type=F32)

    s = jnp.dot(kc_ref[...], qbd, preferred_element_type=F32)
    m = jnp.max(s, axis=0, keepdims=True)
    p = jnp.exp2(s - m)
    state = (m, jnp.sum(p, axis=0, keepdims=True), jnp.ones_like(m),
             jnp.dot(vct_ref[...], p.astype(MXU_DTYPE), preferred_element_type=F32))

    s_bufs, p_bufs = (sa_ref, sb_ref), (pa_ref, pb_ref)
    scores(0, s_bufs[0])
    for c in range(n_chunks):
        if c + 1 < n_chunks:
            scores(c + 1, s_bufs[(c + 1) % 2])
        if c >= 1:
            state = values(c - 1, p_bufs[(c - 1) % 2], state)
        state = softmax(s_bufs[c % 2], p_bufs[c % 2], state)
    _, l, _, acc = values(n_chunks - 1, p_bufs[(n_chunks - 1) % 2], state)

    lam = (jnp.exp(jnp.sum(l1_ref[...] * l2_ref[...], axis=-1, keepdims=True))
           - jnp.exp(jnp.sum(l3_ref[...] * l4_ref[...], axis=-1, keepdims=True)) + lam_init)
    o = acc[:, :tq] / l[:, :tq] - lam * (acc[:, tq:] / l[:, tq:])
    y = o * lax.rsqrt(jnp.mean(o * o, axis=0, keepdims=True) + EPS) * g_ref[...]
    o_ref[...] = (y * (1.0 - lam_init)).T.astype(o_ref.dtype)


def _attention(z, vt, zc, vct, subln_g, lq1, lk1, lq2, lk2, lam_init, tq, tk):
    l = z.shape[0]
    lc = zc.shape[0]
    nh = DIFF_HEADS
    qb, kb = 2 * nh, 3 * nh
    vec = lambda a: a.reshape(1, -1).astype(F32)
    cst = lambda h, i: (0, 0)
    return pl.pallas_call(
        functools.partial(_attn_kernel, tk, lam_init),
        grid=(nh, l // tq),
        in_specs=[pl.BlockSpec((tq, LANES), lambda h, i: (i, qb + h)),
                  pl.BlockSpec((l, LANES), lambda h, i: (0, kb + h)),
                  pl.BlockSpec((DIFF_V, l), lambda h, i: (h, 0)),
                  pl.BlockSpec((lc, LANES), lambda h, i: (0, h)),
                  pl.BlockSpec((DIFF_V, lc), lambda h, i: (h, 0)),
                  pl.BlockSpec((DIFF_V, 1), cst),
                  pl.BlockSpec((1, DIFF_QK), cst), pl.BlockSpec((1, DIFF_QK), cst),
                  pl.BlockSpec((1, DIFF_QK), cst), pl.BlockSpec((1, DIFF_QK), cst)],
        out_specs=pl.BlockSpec((tq, DIFF_V), lambda h, i: (i, h)),
        out_shape=jax.ShapeDtypeStruct((l, nh * DIFF_V), MXU_DTYPE),
        scratch_shapes=[pltpu.VMEM((tk, 2 * tq), F32), pltpu.VMEM((tk, 2 * tq), F32),
                        pltpu.VMEM((tk, 2 * tq), MXU_DTYPE), pltpu.VMEM((tk, 2 * tq), MXU_DTYPE)],
        compiler_params=_cparams(("arbitrary", "arbitrary")), name="attention",
    )(z, z, vt, zc, vct, subln_g.reshape(-1, 1).astype(F32), vec(lq1), vec(lk1), vec(lq2), vec(lk2))


def _out_proj_kernel(u_ref, v_ref, ws_ref, bs_ref, t_ref, w_ref, x_ref, g1_ref, ng_ref, sh_ref,
                     sc_ref, xn_ref, h2_ref):
    tm, da = u_ref.shape
    chunks = []
    for c in range(tm // CHUNK):
        rows = slice(c * CHUNK, (c + 1) * CHUNK)
        groups = []
        for g in range(GMLP_GROUPS):
            cols = slice(g * CHUNK, (g + 1) * CHUNK)
            mixed = jnp.dot(ws_ref[g], v_ref[rows, cols], preferred_element_type=F32)
            mixed = mixed + bs_ref[:, g:g + 1]
            groups.append((u_ref[rows, cols].astype(F32) * mixed).astype(t_ref.dtype))
        chunks.append(jnp.concatenate(groups, axis=1))
    a_out = jnp.concatenate(chunks, axis=0)
    mix = (jnp.dot(a_out, w_ref[0:da, :], preferred_element_type=F32)
           + jnp.dot(t_ref[...], w_ref[da:, :], preferred_element_type=F32))
    xn = x_ref[...] + g1_ref[...] * mix
    xn_ref[...] = xn
    y = xn * lax.rsqrt(jnp.mean(xn * xn, axis=-1, keepdims=True) + EPS) * ng_ref[...]
    h2_ref[...] = (y * (1.0 + sc_ref[...]) + sh_ref[...]).T.astype(h2_ref.dtype)


def _out_proj(z, ws, bs_t, attn, w_out, x2, g1, ng, sh, sc, tm):
    l, d = x2.shape
    da, dv = GMLP_GROUPS * CHUNK, attn.shape[1]
    cst = lambda i: (0, 0)
    return pl.pallas_call(
        _out_proj_kernel,
        grid=(l // tm,),
        in_specs=[pl.BlockSpec((tm, da), lambda i: (i, 0)),
                  pl.BlockSpec((tm, da), lambda i: (i, 1)),
                  pl.BlockSpec((GMLP_GROUPS, CHUNK, CHUNK), lambda i: (0, 0, 0)),
                  pl.BlockSpec((CHUNK, GMLP_GROUPS), cst),
                  pl.BlockSpec((tm, dv), lambda i: (i, 0)),
                  pl.BlockSpec((da + dv, d), cst),
                  pl.BlockSpec((tm, d), lambda i: (i, 0)),
                  pl.BlockSpec((1, d), cst), pl.BlockSpec((1, d), cst),
                  pl.BlockSpec((1, d), cst), pl.BlockSpec((1, d), cst)],
        out_specs=[pl.BlockSpec((tm, d), lambda i: (i, 0)),
                   pl.BlockSpec((d, tm), lambda i: (0, i))],
        out_shape=[jax.ShapeDtypeStruct((l, d), F32), jax.ShapeDtypeStruct((d, l), MXU_DTYPE)],
        compiler_params=_cparams(("arbitrary",)), name="out_proj",
    )(z, z, ws, bs_t, attn, w_out, x2, g1, ng, sh, sc)


SUBL = 8
N_CAND = 10 * SUBL

_SORT16 = ((0, 1), (2, 3), (0, 2), (1, 3), (1, 2), (4, 5), (6, 7), (4, 6), (5, 7), (5, 6), (0, 4),
           (2, 6), (2, 4), (1, 5), (3, 7), (3, 5), (1, 2), (3, 4), (5, 6), (8, 9), (10, 11), (8, 10),
           (9, 11), (9, 10), (12, 13), (14, 15), (12, 14), (13, 15), (13, 14), (8, 12), (10, 14),
           (10, 12), (9, 13), (11, 15), (11, 13), (9, 10), (11, 12), (13, 14), (0, 8), (4, 12),
           (4, 8), (2, 10), (6, 14), (6, 10), (2, 4), (6, 8), (10, 12), (1, 9), (5, 13), (5, 9),
           (3, 11), (7, 15), (7, 11), (3, 5), (7, 9), (11, 13), (1, 2), (3, 4), (5, 6), (7, 8),
           (9, 10), (11, 12), (13, 14))


def _peer_sel_kernel(ht_ref, wqt_ref, keys_ref, cnt_ref, ea_ref, rk_ref, eb_ref,
                     qp_ref, at_ref, bt_ref, as_ref, bs_ref, cand_ref):
    tm = ht_ref.shape[1]
    qp_ref[...] = jnp.dot(wqt_ref[...], ht_ref[...],
                          preferred_element_type=F32).astype(qp_ref.dtype)
    nh = cnt_ref.shape[0]

    def top16(s, out_ref):
        v = [s[k * SUBL:(k + 1) * SUBL, :] for k in range(N_KEYS // SUBL)]
        for i, j in _SORT16:
            v[i], v[j] = jnp.maximum(v[i], v[j]), jnp.minimum(v[i], v[j])
        for r in range(PEER_TOPK):
            m = jnp.max(v[0], axis=0, keepdims=True)
            out_ref[r:r + 1, :] = m
            hit = v[0] == m
            for k in range(PEER_TOPK - 1 - r):
                v[k] = jnp.where(hit, v[k + 1], v[k])

    def strip(h, c0):
        cols = pl.ds(c0, LANES)
        a = at_ref[:, cols]
        b = bt_ref[:, cols]
        top16(a, as_ref)
        top16(b, bs_ref)
        cand_ref[0:PEER_TOPK, :] = as_ref[0:1, :] + bs_ref[...]
        for i in range(1, SUBL):
            r0 = PEER_TOPK + SUBL * (i - 1)
            cand_ref[r0:r0 + SUBL, :] = as_ref[i:i + 1, :] + bs_ref[0:SUBL, :]
        cand_ref[N_CAND - SUBL:N_CAND, :] = as_ref[SUBL:PEER_TOPK, :] + bs_ref[0:1, :]
        c = cand_ref[...]
        cmax = jnp.max(c, axis=0, keepdims=True)
        m = cmax
        zsum = jnp.ones_like(cmax)
        for r in range(1, PEER_TOPK):
            c = jnp.where(c == m, NEG_BIG, c)
            m = jnp.max(c, axis=0, keepdims=True)
            zsum = zsum + jnp.exp(m - cmax)
        tau = m
        bs = bs_ref[...]
        cnt = jnp.zeros(a.shape, F32)
        rank = jnp.full(b.shape, float(PEER_TOPK), F32)
        for r in range(PEER_TOPK):
            ar = as_ref[r:r + 1, :]
            n_r = jnp.sum(jnp.where(ar + bs >= tau, 1.0, 0.0), axis=0, keepdims=True)
            cnt = jnp.where(a == ar, n_r, cnt)
            rank = jnp.where(b == bs_ref[r:r + 1, :], float(r), rank)
        cnt_ref[h, :, cols] = cnt
        ea_ref[h, :, cols] = jnp.exp(a - as_ref[0:1, :])
        rk_ref[h, :, cols] = rank.astype(rk_ref.dtype)
        eb_ref[h, :, cols] = (jnp.exp(b - bs_ref[0:1, :]) / zsum).astype(eb_ref.dtype)

    def head(h, carry):
        ra = pl.multiple_of(2 * h * N_KEYS, N_KEYS)
        rb = pl.multiple_of((2 * h + 1) * N_KEYS, N_KEYS)
        at_ref[...] = jnp.dot(keys_ref[2 * h], qp_ref[pl.ds(ra, N_KEYS), :],
                              preferred_element_type=F32)
        bt_ref[...] = jnp.dot(keys_ref[2 * h + 1], qp_ref[pl.ds(rb, N_KEYS), :],
                              preferred_element_type=F32)

        def body(s, carry):
            strip(h, pl.multiple_of(s * LANES, LANES))
            return carry

        return lax.fori_loop(0, tm // LANES, body, carry)

    lax.fori_loop(0, nh, head, 0)


def _peer_sel(h2t, wqt, keys, tm):
    d, l = h2t.shape
    nh = keys.shape[0] // 2
    blk = pl.BlockSpec((nh, N_KEYS, tm), lambda i: (0, 0, i))
    by_a = jax.ShapeDtypeStruct((nh, N_KEYS, l), F32)
    by_b = jax.ShapeDtypeStruct((nh, N_KEYS, l), MXU_DTYPE)
    return pl.pallas_call(
        _peer_sel_kernel,
        grid=(l // tm,),
        in_specs=[pl.BlockSpec((d, tm), lambda i: (0, i)),
                  pl.BlockSpec(wqt.shape, lambda i: (0, 0)),
                  pl.BlockSpec(keys.shape, lambda i: (0, 0, 0))],
        out_specs=[blk, blk, blk, blk],
        out_shape=[by_a, by_a, by_b, by_b],
        scratch_shapes=[pltpu.VMEM((wqt.shape[0], tm), MXU_DTYPE),
                        pltpu.VMEM((N_KEYS, tm), F32), pltpu.VMEM((N_KEYS, tm), F32),
                        pltpu.VMEM((PEER_TOPK, LANES), F32), pltpu.VMEM((PEER_TOPK, LANES), F32),
                        pltpu.VMEM((N_CAND, LANES), F32)],
        compiler_params=_cparams(("arbitrary",)), name="peer_sel",
    )(h2t, wqt, keys)


def _peer_dense_kernel(ht_ref, u_ref, vt_ref, cnt_ref, ea_ref, rk_ref, eb_ref, x_ref, g2_ref, o_ref,
                       act_ref, acc_ref):
    e = pl.program_id(1)
    n_tiles = pl.num_programs(1) - 1
    nh = cnt_ref.shape[0]
    ni = u_ref.shape[0] // N_KEYS
    gdt = rk_ref.dtype
    tile_rows = 16

    def row_bcast(row):
        one = jnp.broadcast_to(row, (tile_rows, row.shape[1])).astype(gdt)
        return jnp.concatenate([one] * (N_KEYS // tile_rows), axis=0)

    @pl.when(e == 0)
    def _():
        acc_ref[...] = jnp.zeros_like(acc_ref)
        act_ref[...] = jnp.dot(u_ref[...], ht_ref[...], preferred_element_type=F32)

    @pl.when(e > 0)
    def _():
        act = _gelu(act_ref[...]).astype(gdt)
        act_ref[...] = jnp.dot(u_ref[...], ht_ref[...], preferred_element_type=F32)
        ws = []
        for il in range(ni):
            i_glob = (e - 1) * ni + il
            g = None
            for h in range(nh):
                cnt = row_bcast(cnt_ref[h, pl.ds(i_glob, 1), :])
                ea = row_bcast(ea_ref[h, pl.ds(i_glob, 1), :])
                t = jnp.where(rk_ref[h] < cnt, eb_ref[h], jnp.zeros((), gdt)) * ea
                g = t if g is None else g + t
            ws.append(g * act[il * N_KEYS:(il + 1) * N_KEYS, :])
        w = jnp.concatenate(ws, axis=0)
        acc_ref[...] += jnp.dot(vt_ref[...], w, preferred_element_type=F32)

    @pl.when(e == n_tiles)
    def _():
        o_ref[...] = x_ref[...] + g2_ref[...] * acc_ref[...].T


def _peer_dense(h2t, u, vt, cnt, ea, rk, eb, xn, g2, tt, te):
    d, l = h2t.shape
    n_tiles = u.shape[0] // te
    nh = cnt.shape[0]
    sel = pl.BlockSpec((nh, N_KEYS, tt), lambda i, e: (0, 0, i))
    return pl.pallas_call(
        _peer_dense_kernel,
        grid=(l // tt, n_tiles + 1),
        in_specs=[pl.BlockSpec((d, tt), lambda i, e: (0, i)),
                  pl.BlockSpec((te, d), lambda i, e: (jnp.minimum(e, n_tiles - 1), 0)),
                  pl.BlockSpec((d, te), lambda i, e: (0, jnp.maximum(e - 1, 0))),
                  sel, sel, sel, sel,
                  pl.BlockSpec((tt, d), lambda i, e: (i, 0)),
                  pl.BlockSpec((1, d), lambda i, e: (0, 0))],
        out_specs=pl.BlockSpec((tt, d), lambda i, e: (i, 0)),
        out_shape=jax.ShapeDtypeStruct((l, d), F32),
        scratch_shapes=[pltpu.VMEM((te, tt), F32), pltpu.VMEM((d, tt), F32)],
        compiler_params=_cparams(("arbitrary", "arbitrary")), name="peer_dense",
    )(h2t, u, vt, cnt, ea, rk, eb, xn, g2)


def _rope_tables(n_tok):
    n_rows = n_tok // GRID_W
    rows = jnp.repeat(jnp.arange(n_rows, dtype=F32), GRID_W)
    cols = jnp.tile(jnp.arange(GRID_W, dtype=F32), n_rows)
    n_freq = DIFF_QK // 4
    inv = ROPE_BASE ** (-jnp.arange(n_freq, dtype=F32) / n_freq)
    ang = jnp.concatenate([rows[:, None] * inv, cols[:, None] * inv], axis=-1)
    ang = jnp.repeat(ang, 2, axis=-1)
    sign = jnp.tile(jnp.array([-1.0, 1.0], F32), DIFF_QK // 2)
    cos = jnp.tile(jnp.cos(ang), (1, LANES // DIFF_QK))
    sin = jnp.tile(jnp.sin(ang) * sign, (1, LANES // DIFF_QK))
    return cos, sin


def kernel(x, c, ctx, c_ctx, w_ada, b_ada, norm1_g, norm2_g, w_in, gmlp_ln_g, gmlp_ws, gmlp_bs,
           q_norm_g, k_norm_g, lambda_q1, lambda_k1, lambda_q2, lambda_k2, subln_g, w_out,
           peer_wq, peer_keys, peer_u, peer_v):
    depth = w_ada.shape[0]
    assert depth == 1 and x.shape[0] == 1
    _, seq, d = x.shape
    lc = ctx.shape[1]
    da = GMLP_GROUPS * CHUNK
    dqk = DIFF_HEADS * 2 * DIFF_QK
    lam_init = 0.8 - 0.6 * math.exp(-0.3 * 0)
    row = lambda a: a.reshape(1, -1).astype(F32)

    cond8 = jnp.zeros((8, d), F32).at[0].set(c[0]).at[1].set(c_ctx)
    mod = _ada(cond8, w_ada[0], b_ada[0])
    sh1, sc1, g1, sh2, sc2, g2 = [mod[0:1, k * d:(k + 1) * d] for k in range(6)]
    sh1c, sc1c = mod[1:2, 0:d], mod[1:2, d:2 * d]

    w_in_b = w_in[0].astype(MXU_DTYPE)
    ones_bd = jnp.kron(jnp.eye(LANES // DIFF_QK, dtype=F32),
                       jnp.ones((DIFF_QK, DIFF_QK), F32)).astype(MXU_DTYPE)
    qg = jnp.tile(row(q_norm_g[0]), (1, dqk // DIFF_QK))
    kg = jnp.tile(row(k_norm_g[0]), (1, dqk // DIFF_QK))
    lng = row(gmlp_ln_g[0])
    cos, sin = _rope_tables(seq)
    q_scale = (DIFF_QK ** -0.5) * math.log2(math.e)

    z, vt = _in_proj(x[0], row(norm1_g[0]), sh1, sc1, w_in_b, lng, qg, kg, cos, sin, ones_bd,
                     ("u", "va", "q", "k", "v"), q_scale, tm=min(512, seq))
    zc, vct = _in_proj(ctx[0], row(norm1_g[0]), sh1c, sc1c, w_in_b[:, 2 * da + dqk:], lng, qg, kg,
                       jnp.ones((lc, LANES), F32), jnp.zeros((lc, LANES), F32), ones_bd,
                       ("k", "v"), 1.0, tm=lc)

    attn = _attention(z, vt, zc, vct, subln_g[0], lambda_q1[0], lambda_k1[0], lambda_q2[0],
                      lambda_k2[0], lam_init, tq=min(256, seq), tk=min(2048, seq))
    xn, h2t = _out_proj(z, gmlp_ws[0].astype(MXU_DTYPE), gmlp_bs[0].T.astype(F32), attn,
                        w_out[0].astype(MXU_DTYPE), x[0], g1, row(norm2_g[0]), sh2, sc2,
                        tm=min(256, seq))

    keys = peer_keys[0].reshape(PEER_HEADS * 2, N_KEYS, -1).astype(MXU_DTYPE)
    cnt, ea, rk, eb = _peer_sel(h2t, peer_wq[0].T.astype(MXU_DTYPE), keys, tm=min(512, seq))
    out = _peer_dense(h2t, peer_u[0].astype(MXU_DTYPE), peer_v[0].T.astype(MXU_DTYPE),
                      cnt, ea, rk, eb, xn, g2, tt=min(512, seq), te=512)
    return out[None]
```

```python
import functools
import math

import jax
import jax.numpy as jnp
from jax import lax
from jax.experimental import pallas as pl
from jax.experimental.pallas import tpu as pltpu

F32 = jnp.float32
MXU_DTYPE = jnp.bfloat16

EPS = 1e-6
GRID_W = 64
ROPE_BASE = 10000.0

GMLP_GROUPS = 8
CHUNK = 128
DIFF_HEADS = 8
DIFF_QK = 64
DIFF_V = 128
PEER_HEADS = 8
N_KEYS = 128
PEER_TOPK = 16

LANES = 128
NEG_BIG = -1e30

VMEM_LIMIT = 56 * 1024 * 1024

ROWS_IN = 512
ATTN_Q = 256
ATTN_KEYS = 2048
ROWS_OUT = 512
SEL_TOKENS = 512
DENSE_TOKENS = 512
DENSE_EXPERTS = 512


def _cparams(sem):
    return pltpu.CompilerParams(dimension_semantics=sem, vmem_limit_bytes=VMEM_LIMIT)


def _gelu(x):
    return 0.5 * x * (1.0 + lax.erf(x * (2.0 ** -0.5)))


def _ada_kernel(c_ref, w_ref, b_ref, o_ref):
    c = c_ref[...]
    s = c * jax.nn.sigmoid(c)
    o_ref[...] = jnp.dot(s.astype(MXU_DTYPE), w_ref[...].astype(MXU_DTYPE),
                         preferred_element_type=F32) + b_ref[...]


def _ada(cond8, w_ada, b_ada):
    d, n = w_ada.shape
    tn = 1024
    return pl.pallas_call(
        _ada_kernel,
        grid=(n // tn,),
        in_specs=[pl.BlockSpec((8, d), lambda j: (0, 0)),
                  pl.BlockSpec((d, tn), lambda j: (0, j)),
                  pl.BlockSpec((1, tn), lambda j: (0, j))],
        out_specs=pl.BlockSpec((8, tn), lambda j: (0, j)),
        out_shape=jax.ShapeDtypeStruct((8, n), F32),
        compiler_params=_cparams(("arbitrary",)), name="ada",
    )(cond8, w_ada, b_ada.reshape(1, n))


def _group_sum_64(xx, ones_bd):
    outs = []
    for b in range(xx.shape[1] // LANES):
        blk = xx[:, b * LANES:(b + 1) * LANES]
        hi = blk.astype(MXU_DTYPE)
        lo = (blk - hi.astype(F32)).astype(MXU_DTYPE)
        outs.append(jnp.dot(hi, ones_bd, preferred_element_type=F32)
                    + jnp.dot(lo, ones_bd, preferred_element_type=F32))
    return jnp.concatenate(outs, axis=1)


def _swap_pairs(x):
    n = x.shape[1]
    lane = lax.broadcasted_iota(jnp.int32, x.shape, 1)
    nxt = pltpu.roll(x, n - 1, 1)
    prv = pltpu.roll(x, 1, 1)
    return jnp.where((lane & 1) == 0, nxt, prv)


def _in_proj_kernel(sections, q_scale,
                    x_ref, ng_ref, sh_ref, sc_ref, w_ref, lng_ref, qg_ref, kg_ref,
                    cos_ref, sin_ref, ones_ref, o_ref, vt_ref):
    x = x_ref[...]
    y = x * lax.rsqrt(jnp.mean(x * x, axis=-1, keepdims=True) + EPS) * ng_ref[...]
    h = (y * (1.0 + sc_ref[...]) + sh_ref[...]).astype(w_ref.dtype)
    tn = w_ref.shape[1] // len(sections)

    def qk_norm_rope(z, g_ref, scale):
        ms = _group_sum_64(z * z, ones_ref[...]) * (1.0 / DIFF_QK)
        y = z * lax.rsqrt(ms + EPS) * g_ref[...]
        reps = tn // LANES
        cos = jnp.concatenate([cos_ref[...]] * reps, axis=1)
        sin = jnp.concatenate([sin_ref[...]] * reps, axis=1)
        y = y * cos + _swap_pairs(y) * sin
        if scale != 1.0:
            y = y * scale
        return y

    for idx, kind in enumerate(sections):
        acc = jnp.dot(h, w_ref[:, idx * tn:(idx + 1) * tn], preferred_element_type=F32)
        cols = slice(idx * tn, (idx + 1) * tn)
        if kind == "u":
            o_ref[:, cols] = _gelu(acc).astype(o_ref.dtype)
        elif kind == "va":
            g = _gelu(acc)
            for b in range(tn // CHUNK):
                blk = g[:, b * CHUNK:(b + 1) * CHUNK]
                mu = jnp.mean(blk, axis=-1, keepdims=True)
                d = blk - mu
                y = d * lax.rsqrt(jnp.mean(d * d, axis=-1, keepdims=True) + EPS)
                y = y * lng_ref[:, b * CHUNK:(b + 1) * CHUNK]
                o_ref[:, idx * tn + b * CHUNK:idx * tn + (b + 1) * CHUNK] = y.astype(o_ref.dtype)
        elif kind == "q":
            o_ref[:, cols] = qk_norm_rope(acc, qg_ref, q_scale).astype(o_ref.dtype)
        elif kind == "k":
            o_ref[:, cols] = qk_norm_rope(acc, kg_ref, 1.0).astype(o_ref.dtype)
        else:
            vt_ref[...] = acc.T.astype(vt_ref.dtype)


def _in_proj(x2, ng, sh, sc, w, lng, qg, kg, cos, sin, ones_bd, sections, q_scale, tm):
    l, d = x2.shape
    n = w.shape[1]
    tn = n // len(sections)
    assert sections[-1] == "v"
    cst = lambda i: (0, 0)
    return pl.pallas_call(
        functools.partial(_in_proj_kernel, sections, q_scale),
        grid=(l // tm,),
        in_specs=[pl.BlockSpec((tm, d), lambda i: (i, 0)),
                  pl.BlockSpec((1, d), cst), pl.BlockSpec((1, d), cst), pl.BlockSpec((1, d), cst),
                  pl.BlockSpec((d, n), cst, pipeline_mode=pl.Buffered(1)),
                  pl.BlockSpec((1, tn), cst), pl.BlockSpec((1, tn), cst), pl.BlockSpec((1, tn), cst),
                  pl.BlockSpec((tm, LANES), lambda i: (i, 0)),
                  pl.BlockSpec((tm, LANES), lambda i: (i, 0)),
                  pl.BlockSpec((LANES, LANES), cst)],
        out_specs=[pl.BlockSpec((tm, n - tn), lambda i: (i, 0)),
                   pl.BlockSpec((tn, tm), lambda i: (0, i))],
        out_shape=[jax.ShapeDtypeStruct((l, n - tn), MXU_DTYPE),
                   jax.ShapeDtypeStruct((tn, l), MXU_DTYPE)],
        compiler_params=_cparams(("arbitrary",)), name="in_proj",
    )(x2, ng, sh, sc, w, lng, qg, kg, cos, sin, ones_bd)


def _attn_kernel(tk, lam_init,
                 q_ref, k_ref, vt_ref, kc_ref, vct_ref, g_ref, l1_ref, l2_ref, l3_ref, l4_ref,
                 o_ref, sa_ref, sb_ref, pa_ref, pb_ref):
    tq = q_ref.shape[0]
    qt = q_ref[...].astype(F32).T.astype(MXU_DTYPE)
    sub = lax.broadcasted_iota(jnp.int32, qt.shape, 0)
    zero = jnp.zeros_like(qt)
    qbd = jnp.concatenate([jnp.where(sub < DIFF_QK, qt, zero),
                           jnp.where(sub >= DIFF_QK, qt, zero)], axis=1)
    n_chunks = k_ref.shape[0] // tk

    def scores(c, s_ref):
        s_ref[...] = jnp.dot(k_ref[c * tk:(c + 1) * tk, :], qbd, preferred_element_type=F32)

    def softmax(s_ref, p_ref, state):
        m, l, _, acc = state
        s = s_ref[...]
        m_new = jnp.maximum(m, jnp.max(s, axis=0, keepdims=True))
        p = jnp.exp2(s - m_new)
        alpha = jnp.exp2(m - m_new)
        l = alpha * l + jnp.sum(p, axis=0, keepdims=True)
        p_ref[...] = p.astype(p_ref.dtype)
        return m_new, l, alpha, acc

    def values(c, p_ref, state):
        m, l, alpha, acc = state
        vt = vt_ref[:, c * tk:(c + 1) * tk]
        return m, l, alpha, alpha * acc + jnp.dot(vt, p_ref[...], preferred_element_type=F32)

    s = jnp.dot(kc_ref[...], qbd, preferred_element_type=F32)
    m = jnp.max(s, axis=0, keepdims=True)
    p = jnp.exp2(s - m)
    state = (m, jnp.sum(p, axis=0, keepdims=True), jnp.ones_like(m),
             jnp.dot(vct_ref[...], p.astype(MXU_DTYPE), preferred_element_type=F32))

    s_bufs, p_bufs = (sa_ref, sb_ref), (pa_ref, pb_ref)
    scores(0, s_bufs[0])
    for c in range(n_chunks):
        if c + 1 < n_chunks:
            scores(c + 1, s_bufs[(c + 1) % 2])
        if c >= 1:
            state = values(c - 1, p_bufs[(c - 1) % 2], state)
        state = softmax(s_bufs[c % 2], p_bufs[c % 2], state)
    _, l, _, acc = values(n_chunks - 1, p_bufs[(n_chunks - 1) % 2], state)

    lam = (jnp.exp(jnp.sum(l1_ref[...] * l2_ref[...], axis=-1, keepdims=True))
           - jnp.exp(jnp.sum(l3_ref[...] * l4_ref[...], axis=-1, keepdims=True)) + lam_init)
    o = acc[:, :tq] / l[:, :tq] - lam * (acc[:, tq:] / l[:, tq:])
    y = o * lax.rsqrt(jnp.mean(o * o, axis=0, keepdims=True) + EPS) * g_ref[...]
    o_ref[...] = (y * (1.0 - lam_init)).T.astype(o_ref.dtype)


def _attention(z, vt, zc, vct, subln_g, lq1, lk1, lq2, lk2, lam_init, tq, tk):
    l = z.shape[0]
    lc = zc.shape[0]
    nh = DIFF_HEADS
    qb, kb = 2 * nh, 3 * nh
    vec = lambda a: a.reshape(1, -1).astype(F32)
    cst = lambda h, i: (0, 0)
    return pl.pallas_call(
        functools.partial(_attn_kernel, tk, lam_init),
        grid=(nh, l // tq),
        in_specs=[pl.BlockSpec((tq, LANES), lambda h, i: (i, qb + h)),
                  pl.BlockSpec((l, LANES), lambda h, i: (0, kb + h)),
                  pl.BlockSpec((DIFF_V, l), lambda h, i: (h, 0)),
                  pl.BlockSpec((lc, LANES), lambda h, i: (0, h)),
                  pl.BlockSpec((DIFF_V, lc), lambda h, i: (h, 0)),
                  pl.BlockSpec((DIFF_V, 1), cst),
                  pl.BlockSpec((1, DIFF_QK), cst), pl.BlockSpec((1, DIFF_QK), cst),
                  pl.BlockSpec((1, DIFF_QK), cst), pl.BlockSpec((1, DIFF_QK), cst)],
        out_specs=pl.BlockSpec((tq, DIFF_V), lambda h, i: (i, h)),
        out_shape=jax.ShapeDtypeStruct((l, nh * DIFF_V), MXU_DTYPE),
        scratch_shapes=[pltpu.VMEM((tk, 2 * tq), F32), pltpu.VMEM((tk, 2 * tq), F32),
                        pltpu.VMEM((tk, 2 * tq), MXU_DTYPE), pltpu.VMEM((tk, 2 * tq), MXU_DTYPE)],
        compiler_params=_cparams(("arbitrary", "arbitrary")), name="attention",
    )(z, z, vt, zc, vct, subln_g.reshape(-1, 1).astype(F32), vec(lq1), vec(lk1), vec(lq2), vec(lk2))


def _out_proj_kernel(u_ref, v_ref, ws_ref, bs_ref, t_ref, w_ref, x_ref, g1_ref, ng_ref, sh_ref,
                     sc_ref, xn_ref, h2_ref):
    tm, da = u_ref.shape
    chunks = []
    for c in range(tm // CHUNK):
        rows = slice(c * CHUNK, (c + 1) * CHUNK)
        groups = []
        for g in range(GMLP_GROUPS):
            cols = slice(g * CHUNK, (g + 1) * CHUNK)
            mixed = jnp.dot(ws_ref[g], v_ref[rows, cols], preferred_element_type=F32)
            mixed = mixed + bs_ref[:, g:g + 1]
            groups.append((u_ref[rows, cols].astype(F32) * mixed).astype(t_ref.dtype))
        chunks.append(jnp.concatenate(groups, axis=1))
    a_out = jnp.concatenate(chunks, axis=0)
    mix = (jnp.dot(a_out, w_ref[0:da, :], preferred_element_type=F32)
           + jnp.dot(t_ref[...], w_ref[da:, :], preferred_element_type=F32))
    xn = x_ref[...] + g1_ref[...] * mix
    xn_ref[...] = xn
    y = xn * lax.rsqrt(jnp.mean(xn * xn, axis=-1, keepdims=True) + EPS) * ng_ref[...]
    h2_ref[...] = (y * (1.0 + sc_ref[...]) + sh_ref[...]).T.astype(h2_ref.dtype)


def _out_proj(z, ws, bs_t, attn, w_out, x2, g1, ng, sh, sc, tm):
    l, d = x2.shape
    da, dv = GMLP_GROUPS * CHUNK, attn.shape[1]
    cst = lambda i: (0, 0)
    return pl.pallas_call(
        _out_proj_kernel,
        grid=(l // tm,),
        in_specs=[pl.BlockSpec((tm, da), lambda i: (i, 0)),
                  pl.BlockSpec((tm, da), lambda i: (i, 1)),
                  pl.BlockSpec((GMLP_GROUPS, CHUNK, CHUNK), lambda i: (0, 0, 0)),
                  pl.BlockSpec((CHUNK, GMLP_GROUPS), cst),
                  pl.BlockSpec((tm, dv), lambda i: (i, 0)),
                  pl.BlockSpec((da + dv, d), cst),
                  pl.BlockSpec((tm, d), lambda i: (i, 0)),
                  pl.BlockSpec((1, d), cst), pl.BlockSpec((1, d), cst),
                  pl.BlockSpec((1, d), cst), pl.BlockSpec((1, d), cst)],
        out_specs=[pl.BlockSpec((tm, d), lambda i: (i, 0)),
                   pl.BlockSpec((d, tm), lambda i: (0, i))],
        out_shape=[jax.ShapeDtypeStruct((l, d), F32), jax.ShapeDtypeStruct((d, l), MXU_DTYPE)],
        compiler_params=_cparams(("arbitrary",)), name="out_proj",
    )(z, z, ws, bs_t, attn, w_out, x2, g1, ng, sh, sc)


SUBL = 8
N_CAND = 10 * SUBL

_SORT16 = ((0, 1), (2, 3), (0, 2), (1, 3), (1, 2), (4, 5), (6, 7), (4, 6), (5, 7), (5, 6), (0, 4),
           (2, 6), (2, 4), (1, 5), (3, 7), (3, 5), (1, 2), (3, 4), (5, 6), (8, 9), (10, 11), (8, 10),
           (9, 11), (9, 10), (12, 13), (14, 15), (12, 14), (13, 15), (13, 14), (8, 12), (10, 14),
           (10, 12), (9, 13), (11, 15), (11, 13), (9, 10), (11, 12), (13, 14), (0, 8), (4, 12),
           (4, 8), (2, 10), (6, 14), (6, 10), (2, 4), (6, 8), (10, 12), (1, 9), (5, 13), (5, 9),
           (3, 11), (7, 15), (7, 11), (3, 5), (7, 9), (11, 13), (1, 2), (3, 4), (5, 6), (7, 8),
           (9, 10), (11, 12), (13, 14))


def _peer_sel_kernel(ht_ref, wqt_ref, keys_ref, cnt_ref, ea_ref, rk_ref, eb_ref,
                     qp_ref, at_ref, bt_ref, as_ref, bs_ref, cand_ref):
    tm = ht_ref.shape[1]
    qp_ref[...] = jnp.dot(wqt_ref[...], ht_ref[...],
                          preferred_element_type=F32).astype(qp_ref.dtype)
    nh = cnt_ref.shape[0]

    def top16(s, out_ref):
        v = [s[k * SUBL:(k + 1) * SUBL, :] for k in range(N_KEYS // SUBL)]
        for i, j in _SORT16:
            v[i], v[j] = jnp.maximum(v[i], v[j]), jnp.minimum(v[i], v[j])
        for r in range(PEER_TOPK):
            m = jnp.max(v[0], axis=0, keepdims=True)
            out_ref[r:r + 1, :] = m
            hit = v[0] == m
            for k in range(PEER_TOPK - 1 - r):
                v[k] = jnp.where(hit, v[k + 1], v[k])

    def strip(h, c0):
        cols = pl.ds(c0, LANES)
        a = at_ref[:, cols]
        b = bt_ref[:, cols]
        top16(a, as_ref)
        top16(b, bs_ref)
        cand_ref[0:PEER_TOPK, :] = as_ref[0:1, :] + bs_ref[...]
        for i in range(1, SUBL):
            r0 = PEER_TOPK + SUBL * (i - 1)
            cand_ref[r0:r0 + SUBL, :] = as_ref[i:i + 1, :] + bs_ref[0:SUBL, :]
        cand_ref[N_CAND - SUBL:N_CAND, :] = as_ref[SUBL:PEER_TOPK, :] + bs_ref[0:1, :]
        c = cand_ref[...]
        cmax = jnp.max(c, axis=0, keepdims=True)
        m = cmax
        zsum = jnp.ones_like(cmax)
        for r in range(1, PEER_TOPK):
            c = jnp.where(c == m, NEG_BIG, c)
            m = jnp.max(c, axis=0, keepdims=True)
            zsum = zsum + jnp.exp(m - cmax)
        tau = m
        bs = bs_ref[...]
        cnt = jnp.zeros(a.shape, F32)
        rank = jnp.full(b.shape, float(PEER_TOPK), F32)
        for r in range(PEER_TOPK):
            ar = as_ref[r:r + 1, :]
            n_r = jnp.sum(jnp.where(ar + bs >= tau, 1.0, 0.0), axis=0, keepdims=True)
            cnt = jnp.where(a == ar, n_r, cnt)
            rank = jnp.where(b == bs_ref[r:r + 1, :], float(r), rank)
        cnt_ref[h, :, cols] = cnt
        ea_ref[h, :, cols] = jnp.exp(a - as_ref[0:1, :])
        rk_ref[h, :, cols] = rank.astype(rk_ref.dtype)
        eb_ref[h, :, cols] = (jnp.exp(b - bs_ref[0:1, :]) / zsum).astype(eb_ref.dtype)

    def head(h, carry):
        ra = pl.multiple_of(2 * h * N_KEYS, N_KEYS)
        rb = pl.multiple_of((2 * h + 1) * N_KEYS, N_KEYS)
        at_ref[...] = jnp.dot(keys_ref[2 * h], qp_ref[pl.ds(ra, N_KEYS), :],
                              preferred_element_type=F32)
        bt_ref[...] = jnp.dot(keys_ref[2 * h + 1], qp_ref[pl.ds(rb, N_KEYS), :],
                              preferred_element_type=F32)

        def body(s, carry):
            strip(h, pl.multiple_of(s * LANES, LANES))
            return carry

        return lax.fori_loop(0, tm // LANES, body, carry)

    lax.fori_loop(0, nh, head, 0)


def _peer_sel(h2t, wqt, keys, tm):
    d, l = h2t.shape
    nh = keys.shape[0] // 2
    blk = pl.BlockSpec((nh, N_KEYS, tm), lambda i: (0, 0, i))
    by_a = jax.ShapeDtypeStruct((nh, N_KEYS, l), F32)
    by_b = jax.ShapeDtypeStruct((nh, N_KEYS, l), MXU_DTYPE)
    return pl.pallas_call(
        _peer_sel_kernel,
        grid=(l // tm,),
        in_specs=[pl.BlockSpec((d, tm), lambda i: (0, i)),
                  pl.BlockSpec(wqt.shape, lambda i: (0, 0)),
                  pl.BlockSpec(keys.shape, lambda i: (0, 0, 0))],
        out_specs=[blk, blk, blk, blk],
        out_shape=[by_a, by_a, by_b, by_b],
        scratch_shapes=[pltpu.VMEM((wqt.shape[0], tm), MXU_DTYPE),
                        pltpu.VMEM((N_KEYS, tm), F32), pltpu.VMEM((N_KEYS, tm), F32),
                        pltpu.VMEM((PEER_TOPK, LANES), F32), pltpu.VMEM((PEER_TOPK, LANES), F32),
                        pltpu.VMEM((N_CAND, LANES), F32)],
        compiler_params=_cparams(("arbitrary",)), name="peer_sel",
    )(h2t, wqt, keys)


def _peer_dense_kernel(ht_ref, u_ref, vt_ref, cnt_ref, ea_ref, rk_ref, eb_ref, x_ref, g2_ref, o_ref,
                       act_ref, acc_ref):
    e = pl.program_id(1)
    n_tiles = pl.num_programs(1) - 1
    nh = cnt_ref.shape[0]
    ni = u_ref.shape[0] // N_KEYS
    gdt = rk_ref.dtype
    tile_rows = 16

    def row_bcast(row):
        one = jnp.broadcast_to(row, (tile_rows, row.shape[1])).astype(gdt)
        return jnp.concatenate([one] * (N_KEYS // tile_rows), axis=0)

    @pl.when(e == 0)
    def _():
        acc_ref[...] = jnp.zeros_like(acc_ref)
        act_ref[...] = jnp.dot(u_ref[...], ht_ref[...], preferred_element_type=F32)

    @pl.when(e > 0)
    def _():
        act = _gelu(act_ref[...]).astype(gdt)
        act_ref[...] = jnp.dot(u_ref[...], ht_ref[...], preferred_element_type=F32)
        ws = []
        for il in range(ni):
            i_glob = (e - 1) * ni + il
            g = None
            for h in range(nh):
                cnt = row_bcast(cnt_ref[h, pl.ds(i_glob, 1), :])
                ea = row_bcast(ea_ref[h, pl.ds(i_glob, 1), :])
                t = jnp.where(rk_ref[h] < cnt, eb_ref[h], jnp.zeros((), gdt)) * ea
                g = t if g is None else g + t
            ws.append(g * act[il * N_KEYS:(il + 1) * N_KEYS, :])
        w = jnp.concatenate(ws, axis=0)
        acc_ref[...] += jnp.dot(vt_ref[...], w, preferred_element_type=F32)

    @pl.when(e == n_tiles)
    def _():
        o_ref[...] = x_ref[...] + g2_ref[...] * acc_ref[...].T


def _peer_dense(h2t, u, vt, cnt, ea, rk, eb, xn, g2, tt, te):
    d, l = h2t.shape
    n_tiles = u.shape[0] // te
    nh = cnt.shape[0]
    sel = pl.BlockSpec((nh, N_KEYS, tt), lambda i, e: (0, 0, i))
    return pl.pallas_call(
        _peer_dense_kernel,
        grid=(l // tt, n_tiles + 1),
        in_specs=[pl.BlockSpec((d, tt), lambda i, e: (0, i)),
                  pl.BlockSpec((te, d), lambda i, e: (jnp.minimum(e, n_tiles - 1), 0)),
                  pl.BlockSpec((d, te), lambda i, e: (0, jnp.maximum(e - 1, 0))),
                  sel, sel, sel, sel,
                  pl.BlockSpec((tt, d), lambda i, e: (i, 0)),
                  pl.BlockSpec((1, d), lambda i, e: (0, 0))],
        out_specs=pl.BlockSpec((tt, d), lambda i, e: (i, 0)),
        out_shape=jax.ShapeDtypeStruct((l, d), F32),
        scratch_shapes=[pltpu.VMEM((te, tt), F32), pltpu.VMEM((d, tt), F32)],
        compiler_params=_cparams(("arbitrary", "arbitrary")), name="peer_dense",
    )(h2t, u, vt, cnt, ea, rk, eb, xn, g2)


def _rope_tables(n_tok):
    n_rows = n_tok // GRID_W
    rows = jnp.repeat(jnp.arange(n_rows, dtype=F32), GRID_W)
    cols = jnp.tile(jnp.arange(GRID_W, dtype=F32), n_rows)
    n_freq = DIFF_QK // 4
    inv = ROPE_BASE ** (-jnp.arange(n_freq, dtype=F32) / n_freq)
    ang = jnp.concatenate([rows[:, None] * inv, cols[:, None] * inv], axis=-1)
    ang = jnp.repeat(ang, 2, axis=-1)
    sign = jnp.tile(jnp.array([-1.0, 1.0], F32), DIFF_QK // 2)
    cos = jnp.tile(jnp.cos(ang), (1, LANES // DIFF_QK))
    sin = jnp.tile(jnp.sin(ang) * sign, (1, LANES // DIFF_QK))
    return cos, sin


def kernel(x, c, ctx, c_ctx, w_ada, b_ada, norm1_g, norm2_g, w_in, gmlp_ln_g, gmlp_ws, gmlp_bs,
           q_norm_g, k_norm_g, lambda_q1, lambda_k1, lambda_q2, lambda_k2, subln_g, w_out,
           peer_wq, peer_keys, peer_u, peer_v):
    depth = w_ada.shape[0]
    assert depth == 1 and x.shape[0] == 1
    _, seq, d = x.shape
    lc = ctx.shape[1]
    da = GMLP_GROUPS * CHUNK
    dqk = DIFF_HEADS * 2 * DIFF_QK
    lam_init = 0.8 - 0.6 * math.exp(-0.3 * 0)
    row = lambda a: a.reshape(1, -1).astype(F32)

    cond8 = jnp.zeros((8, d), F32).at[0].set(c[0]).at[1].set(c_ctx)
    mod = _ada(cond8, w_ada[0], b_ada[0])
    sh1, sc1, g1, sh2, sc2, g2 = [mod[0:1, k * d:(k + 1) * d] for k in range(6)]
    sh1c, sc1c = mod[1:2, 0:d], mod[1:2, d:2 * d]

    w_in_b = w_in[0].astype(MXU_DTYPE)
    ones_bd = jnp.kron(jnp.eye(LANES // DIFF_QK, dtype=F32),
                       jnp.ones((DIFF_QK, DIFF_QK), F32)).astype(MXU_DTYPE)
    qg = jnp.tile(row(q_norm_g[0]), (1, dqk // DIFF_QK))
    kg = jnp.tile(row(k_norm_g[0]), (1, dqk // DIFF_QK))
    lng = row(gmlp_ln_g[0])
    cos, sin = _rope_tables(seq)
    q_scale = (DIFF_QK ** -0.5) * math.log2(math.e)

    z, vt = _in_proj(x[0], row(norm1_g[0]), sh1, sc1, w_in_b, lng, qg, kg, cos, sin, ones_bd,
                     ("u", "va", "q", "k", "v"), q_scale, tm=min(ROWS_IN, seq))
    zc, vct = _in_proj(ctx[0], row(norm1_g[0]), sh1c, sc1c, w_in_b[:, 2 * da + dqk:], lng, qg, kg,
                       jnp.ones((lc, LANES), F32), jnp.zeros((lc, LANES), F32), ones_bd,
                       ("k", "v"), 1.0, tm=lc)

    attn = _attention(z, vt, zc, vct, subln_g[0], lambda_q1[0], lambda_k1[0], lambda_q2[0],
                      lambda_k2[0], lam_init, tq=min(ATTN_Q, seq), tk=min(ATTN_KEYS, seq))
    xn, h2t = _out_proj(z, gmlp_ws[0].astype(MXU_DTYPE), gmlp_bs[0].T.astype(F32), attn,
                        w_out[0].astype(MXU_DTYPE), x[0], g1, row(norm2_g[0]), sh2, sc2,
                        tm=min(ROWS_OUT, seq))

    keys = peer_keys[0].reshape(PEER_HEADS * 2, N_KEYS, -1).astype(MXU_DTYPE)
    cnt, ea, rk, eb = _peer_sel(h2t, peer_wq[0].T.astype(MXU_DTYPE), keys,
                                tm=min(SEL_TOKENS, seq))
    out = _peer_dense(h2t, peer_u[0].astype(MXU_DTYPE), peer_v[0].T.astype(MXU_DTYPE),
                      cnt, ea, rk, eb, xn, g2, tt=min(DENSE_TOKENS, seq), te=DENSE_EXPERTS)
    return out[None]
```

```python
import functools
import math

import jax
import jax.numpy as jnp
from jax import lax
from jax.experimental import pallas as pl
from jax.experimental.pallas import tpu as pltpu

F32 = jnp.float32
MXU_DTYPE = jnp.bfloat16

EPS = 1e-6
GRID_W = 64
ROPE_BASE = 10000.0

GMLP_GROUPS = 8
CHUNK = 128
DIFF_HEADS = 8
DIFF_QK = 64
DIFF_V = 128
PEER_HEADS = 8
N_KEYS = 128
PEER_TOPK = 16

LANES = 128
NEG_BIG = -1e30

VMEM_LIMIT = 56 * 1024 * 1024

ROWS_IN = 512
ATTN_Q = 256
ATTN_KEYS = 1024
ROWS_OUT = 512
SEL_TOKENS = 512
DENSE_TOKENS = 512
DENSE_EXPERTS = 512


def _cparams(sem):
    return pltpu.CompilerParams(dimension_semantics=sem, vmem_limit_bytes=VMEM_LIMIT)


def _gelu(x):
    return 0.5 * x * (1.0 + lax.erf(x * (2.0 ** -0.5)))


def _ada_kernel(c_ref, w_ref, b_ref, o_ref):
    c = c_ref[...]
    s = c * jax.nn.sigmoid(c)
    o_ref[...] = jnp.dot(s.astype(MXU_DTYPE), w_ref[...].astype(MXU_DTYPE),
                         preferred_element_type=F32) + b_ref[...]


def _ada(cond8, w_ada, b_ada):
    d, n = w_ada.shape
    tn = 1024
    return pl.pallas_call(
        _ada_kernel,
        grid=(n // tn,),
        in_specs=[pl.BlockSpec((8, d), lambda j: (0, 0)),
                  pl.BlockSpec((d, tn), lambda j: (0, j)),
                  pl.BlockSpec((1, tn), lambda j: (0, j))],
        out_specs=pl.BlockSpec((8, tn), lambda j: (0, j)),
        out_shape=jax.ShapeDtypeStruct((8, n), F32),
        compiler_params=_cparams(("arbitrary",)), name="ada",
    )(cond8, w_ada, b_ada.reshape(1, n))


def _group_sum_64(xx, ones_bd):
    outs = []
    for b in range(xx.shape[1] // LANES):
        blk = xx[:, b * LANES:(b + 1) * LANES]
        hi = blk.astype(MXU_DTYPE)
        lo = (blk - hi.astype(F32)).astype(MXU_DTYPE)
        outs.append(jnp.dot(hi, ones_bd, preferred_element_type=F32)
                    + jnp.dot(lo, ones_bd, preferred_element_type=F32))
    return jnp.concatenate(outs, axis=1)


def _swap_pairs(x):
    n = x.shape[1]
    lane = lax.broadcasted_iota(jnp.int32, x.shape, 1)
    nxt = pltpu.roll(x, n - 1, 1)
    prv = pltpu.roll(x, 1, 1)
    return jnp.where((lane & 1) == 0, nxt, prv)


def _in_proj_kernel(sections, q_scale,
                    x_ref, ng_ref, sh_ref, sc_ref, w_ref, lng_ref, qg_ref, kg_ref,
                    cos_ref, sin_ref, ones_ref, o_ref, vt_ref):
    x = x_ref[...]
    y = x * lax.rsqrt(jnp.mean(x * x, axis=-1, keepdims=True) + EPS) * ng_ref[...]
    h = (y * (1.0 + sc_ref[...]) + sh_ref[...]).astype(w_ref.dtype)
    tn = w_ref.shape[1] // len(sections)

    def qk_norm_rope(z, g_ref, scale):
        ms = _group_sum_64(z * z, ones_ref[...]) * (1.0 / DIFF_QK)
        y = z * lax.rsqrt(ms + EPS) * g_ref[...]
        reps = tn // LANES
        cos = jnp.concatenate([cos_ref[...]] * reps, axis=1)
        sin = jnp.concatenate([sin_ref[...]] * reps, axis=1)
        y = y * cos + _swap_pairs(y) * sin
        if scale != 1.0:
            y = y * scale
        return y

    for idx, kind in enumerate(sections):
        acc = jnp.dot(h, w_ref[:, idx * tn:(idx + 1) * tn], preferred_element_type=F32)
        cols = slice(idx * tn, (idx + 1) * tn)
        if kind == "u":
            o_ref[:, cols] = _gelu(acc).astype(o_ref.dtype)
        elif kind == "va":
            g = _gelu(acc)
            for b in range(tn // CHUNK):
                blk = g[:, b * CHUNK:(b + 1) * CHUNK]
                mu = jnp.mean(blk, axis=-1, keepdims=True)
                d = blk - mu
                y = d * lax.rsqrt(jnp.mean(d * d, axis=-1, keepdims=True) + EPS)
                y = y * lng_ref[:, b * CHUNK:(b + 1) * CHUNK]
                o_ref[:, idx * tn + b * CHUNK:idx * tn + (b + 1) * CHUNK] = y.astype(o_ref.dtype)
        elif kind == "q":
            o_ref[:, cols] = qk_norm_rope(acc, qg_ref, q_scale).astype(o_ref.dtype)
        elif kind == "k":
            o_ref[:, cols] = qk_norm_rope(acc, kg_ref, 1.0).astype(o_ref.dtype)
        else:
            vt_ref[...] = acc.T.astype(vt_ref.dtype)


def _in_proj(x2, ng, sh, sc, w, lng, qg, kg, cos, sin, ones_bd, sections, q_scale, tm):
    l, d = x2.shape
    n = w.shape[1]
    tn = n // len(sections)
    assert sections[-1] == "v"
    cst = lambda i: (0, 0)
    return pl.pallas_call(
        functools.partial(_in_proj_kernel, sections, q_scale),
        grid=(l // tm,),
        in_specs=[pl.BlockSpec((tm, d), lambda i: (i, 0)),
                  pl.BlockSpec((1, d), cst), pl.BlockSpec((1, d), cst), pl.BlockSpec((1, d), cst),
                  pl.BlockSpec((d, n), cst, pipeline_mode=pl.Buffered(1)),
                  pl.BlockSpec((1, tn), cst), pl.BlockSpec((1, tn), cst), pl.BlockSpec((1, tn), cst),
                  pl.BlockSpec((tm, LANES), lambda i: (i, 0)),
                  pl.BlockSpec((tm, LANES), lambda i: (i, 0)),
                  pl.BlockSpec((LANES, LANES), cst)],
        out_specs=[pl.BlockSpec((tm, n - tn), lambda i: (i, 0)),
                   pl.BlockSpec((tn, tm), lambda i: (0, i))],
        out_shape=[jax.ShapeDtypeStruct((l, n - tn), MXU_DTYPE),
                   jax.ShapeDtypeStruct((tn, l), MXU_DTYPE)],
        compiler_params=_cparams(("arbitrary",)), name="in_proj",
    )(x2, ng, sh, sc, w, lng, qg, kg, cos, sin, ones_bd)


def _attn_kernel(tk, lam_init,
                 q_ref, k_ref, vt_ref, kc_ref, vct_ref, g_ref, l1_ref, l2_ref, l3_ref, l4_ref,
                 o_ref, sa_ref, sb_ref, pa_ref, pb_ref):
    tq = q_ref.shape[0]
    qt = q_ref[...].astype(F32).T.astype(MXU_DTYPE)
    sub = lax.broadcasted_iota(jnp.int32, qt.shape, 0)
    zero = jnp.zeros_like(qt)
    qbd = jnp.concatenate([jnp.where(sub < DIFF_QK, qt, zero),
                           jnp.where(sub >= DIFF_QK, qt, zero)], axis=1)
    n_chunks = k_ref.shape[0] // tk

    def scores(c, s_ref):
        s_ref[...] = jnp.dot(k_ref[c * tk:(c + 1) * tk, :], qbd, preferred_element_type=F32)

    def softmax(s_ref, p_ref, state):
        m, l, _, acc = state
        s = s_ref[...]
        m_new = jnp.maximum(m, jnp.max(s, axis=0, keepdims=True))
        p = jnp.exp2(s - m_new)
        alpha = jnp.exp2(m - m_new)
        l = alpha * l + jnp.sum(p, axis=0, keepdims=True)
        p_ref[...] = p.astype(p_ref.dtype)
        return m_new, l, alpha, acc

    def values(c, p_ref, state):
        m, l, alpha, acc = state
        vt = vt_ref[:, c * tk:(c + 1) * tk]
        return m, l, alpha, alpha * acc + jnp.dot(vt, p_ref[...], preferred_element_type=F32)

    s = jnp.dot(kc_ref[...], qbd, preferred_element_type=F32)
    m = jnp.max(s, axis=0, keepdims=True)
    p = jnp.exp2(s - m)
    state = (m, jnp.sum(p, axis=0, keepdims=True), jnp.ones_like(m),
             jnp.dot(vct_ref[...], p.astype(MXU_DTYPE), preferred_element_type=F32))

    s_bufs, p_bufs = (sa_ref, sb_ref), (pa_ref, pb_ref)
    scores(0, s_bufs[0])
    for c in range(n_chunks):
        if c + 1 < n_chunks:
            scores(c + 1, s_bufs[(c + 1) % 2])
        state = softmax(s_bufs[c % 2], p_bufs[c % 2], state)
        state = values(c, p_bufs[c % 2], state)
    _, l, _, acc = state

    lam = (jnp.exp(jnp.sum(l1_ref[...] * l2_ref[...], axis=-1, keepdims=True))
           - jnp.exp(jnp.sum(l3_ref[...] * l4_ref[...], axis=-1, keepdims=True)) + lam_init)
    o = acc[:, :tq] / l[:, :tq] - lam * (acc[:, tq:] / l[:, tq:])
    y = o * lax.rsqrt(jnp.mean(o * o, axis=0, keepdims=True) + EPS) * g_ref[...]
    o_ref[...] = (y * (1.0 - lam_init)).T.astype(o_ref.dtype)


def _attention(z, vt, zc, vct, subln_g, lq1, lk1, lq2, lk2, lam_init, tq, tk):
    l = z.shape[0]
    lc = zc.shape[0]
    nh = DIFF_HEADS
    qb, kb = 2 * nh, 3 * nh
    vec = lambda a: a.reshape(1, -1).astype(F32)
    cst = lambda h, i: (0, 0)
    return pl.pallas_call(
        functools.partial(_attn_kernel, tk, lam_init),
        grid=(nh, l // tq),
        in_specs=[pl.BlockSpec((tq, LANES), lambda h, i: (i, qb + h)),
                  pl.BlockSpec((l, LANES), lambda h, i: (0, kb + h)),
                  pl.BlockSpec((DIFF_V, l), lambda h, i: (h, 0)),
                  pl.BlockSpec((lc, LANES), lambda h, i: (0, h)),
                  pl.BlockSpec((DIFF_V, lc), lambda h, i: (h, 0)),
                  pl.BlockSpec((DIFF_V, 1), cst),
                  pl.BlockSpec((1, DIFF_QK), cst), pl.BlockSpec((1, DIFF_QK), cst),
                  pl.BlockSpec((1, DIFF_QK), cst), pl.BlockSpec((1, DIFF_QK), cst)],
        out_specs=pl.BlockSpec((tq, DIFF_V), lambda h, i: (i, h)),
        out_shape=jax.ShapeDtypeStruct((l, nh * DIFF_V), MXU_DTYPE),
        scratch_shapes=[pltpu.VMEM((tk, 2 * tq), F32), pltpu.VMEM((tk, 2 * tq), F32),
                        pltpu.VMEM((tk, 2 * tq), MXU_DTYPE), pltpu.VMEM((tk, 2 * tq), MXU_DTYPE)],
        compiler_params=_cparams(("arbitrary", "arbitrary")), name="attention",
    )(z, z, vt, zc, vct, subln_g.reshape(-1, 1).astype(F32), vec(lq1), vec(lk1), vec(lq2), vec(lk2))


def _out_proj_kernel(u_ref, v_ref, ws_ref, bs_ref, t_ref, w_ref, x_ref, g1_ref, ng_ref, sh_ref,
                     sc_ref, xn_ref, h2_ref):
    tm, da = u_ref.shape
    chunks = []
    for c in range(tm // CHUNK):
        rows = slice(c * CHUNK, (c + 1) * CHUNK)
        groups = []
        for g in range(GMLP_GROUPS):
            cols = slice(g * CHUNK, (g + 1) * CHUNK)
            mixed = jnp.dot(ws_ref[g], v_ref[rows, cols], preferred_element_type=F32)
            mixed = mixed + bs_ref[:, g:g + 1]
            groups.append((u_ref[rows, cols].astype(F32) * mixed).astype(t_ref.dtype))
        chunks.append(jnp.concatenate(groups, axis=1))
    a_out = jnp.concatenate(chunks, axis=0)
    mix = (jnp.dot(a_out, w_ref[0:da, :], preferred_element_type=F32)
           + jnp.dot(t_ref[...], w_ref[da:, :], preferred_element_type=F32))
    xn = x_ref[...] + g1_ref[...] * mix
    xn_ref[...] = xn
    y = xn * lax.rsqrt(jnp.mean(xn * xn, axis=-1, keepdims=True) + EPS) * ng_ref[...]
    h2_ref[...] = (y * (1.0 + sc_ref[...]) + sh_ref[...]).T.astype(h2_ref.dtype)


def _out_proj(z, ws, bs_t, attn, w_out, x2, g1, ng, sh, sc, tm):
    l, d = x2.shape
    da, dv = GMLP_GROUPS * CHUNK, attn.shape[1]
    cst = lambda i: (0, 0)
    return pl.pallas_call(
        _out_proj_kernel,
        grid=(l // tm,),
        in_specs=[pl.BlockSpec((tm, da), lambda i: (i, 0)),
                  pl.BlockSpec((tm, da), lambda i: (i, 1)),
                  pl.BlockSpec((GMLP_GROUPS, CHUNK, CHUNK), lambda i: (0, 0, 0)),
                  pl.BlockSpec((CHUNK, GMLP_GROUPS), cst),
                  pl.BlockSpec((tm, dv), lambda i: (i, 0)),
                  pl.BlockSpec((da + dv, d), cst),
                  pl.BlockSpec((tm, d), lambda i: (i, 0)),
                  pl.BlockSpec((1, d), cst), pl.BlockSpec((1, d), cst),
                  pl.BlockSpec((1, d), cst), pl.BlockSpec((1, d), cst)],
        out_specs=[pl.BlockSpec((tm, d), lambda i: (i, 0)),
                   pl.BlockSpec((d, tm), lambda i: (0, i))],
        out_shape=[jax.ShapeDtypeStruct((l, d), F32), jax.ShapeDtypeStruct((d, l), MXU_DTYPE)],
        compiler_params=_cparams(("arbitrary",)), name="out_proj",
    )(z, z, ws, bs_t, attn, w_out, x2, g1, ng, sh, sc)


SUBL = 8
N_CAND = 10 * SUBL

_SORT16 = ((0, 1), (2, 3), (0, 2), (1, 3), (1, 2), (4, 5), (6, 7), (4, 6), (5, 7), (5, 6), (0, 4),
           (2, 6), (2, 4), (1, 5), (3, 7), (3, 5), (1, 2), (3, 4), (5, 6), (8, 9), (10, 11), (8, 10),
           (9, 11), (9, 10), (12, 13), (14, 15), (12, 14), (13, 15), (13, 14), (8, 12), (10, 14),
           (10, 12), (9, 13), (11, 15), (11, 13), (9, 10), (11, 12), (13, 14), (0, 8), (4, 12),
           (4, 8), (2, 10), (6, 14), (6, 10), (2, 4), (6, 8), (10, 12), (1, 9), (5, 13), (5, 9),
           (3, 11), (7, 15), (7, 11), (3, 5), (7, 9), (11, 13), (1, 2), (3, 4), (5, 6), (7, 8),
           (9, 10), (11, 12), (13, 14))


def _peer_sel_kernel(ht_ref, wqt_ref, keys_ref, cnt_ref, ea_ref, rk_ref, eb_ref,
                     qp_ref, at_ref, bt_ref, as_ref, bs_ref, cand_ref):
    tm = ht_ref.shape[1]
    qp_ref[...] = jnp.dot(wqt_ref[...], ht_ref[...],
                          preferred_element_type=F32).astype(qp_ref.dtype)
    nh = cnt_ref.shape[0]

    def top16(s, out_ref):
        v = [s[k * SUBL:(k + 1) * SUBL, :] for k in range(N_KEYS // SUBL)]
        for i, j in _SORT16:
            v[i], v[j] = jnp.maximum(v[i], v[j]), jnp.minimum(v[i], v[j])
        for r in range(PEER_TOPK):
            m = jnp.max(v[0], axis=0, keepdims=True)
            out_ref[r:r + 1, :] = m
            hit = v[0] == m
            for k in range(PEER_TOPK - 1 - r):
                v[k] = jnp.where(hit, v[k + 1], v[k])

    def strip(h, c0):
        cols = pl.ds(c0, LANES)
        a = at_ref[:, cols]
        b = bt_ref[:, cols]
        top16(a, as_ref)
        top16(b, bs_ref)
        cand_ref[0:PEER_TOPK, :] = as_ref[0:1, :] + bs_ref[...]
        for i in range(1, SUBL):
            r0 = PEER_TOPK + SUBL * (i - 1)
            cand_ref[r0:r0 + SUBL, :] = as_ref[i:i + 1, :] + bs_ref[0:SUBL, :]
        cand_ref[N_CAND - SUBL:N_CAND, :] = as_ref[SUBL:PEER_TOPK, :] + bs_ref[0:1, :]
        c = cand_ref[...]
        cmax = jnp.max(c, axis=0, keepdims=True)
        m = cmax
        zsum = jnp.ones_like(cmax)
        for r in range(1, PEER_TOPK):
            c = jnp.where(c == m, NEG_BIG, c)
            m = jnp.max(c, axis=0, keepdims=True)
            zsum = zsum + jnp.exp(m - cmax)
        tau = m
        bs = bs_ref[...]
        cnt = jnp.zeros(a.shape, F32)
        rank = jnp.full(b.shape, float(PEER_TOPK), F32)
        for r in range(PEER_TOPK):
            ar = as_ref[r:r + 1, :]
            n_r = jnp.sum(jnp.where(ar + bs >= tau, 1.0, 0.0), axis=0, keepdims=True)
            cnt = jnp.where(a == ar, n_r, cnt)
            rank = jnp.where(b == bs_ref[r:r + 1, :], float(r), rank)
        cnt_ref[h, :, cols] = cnt
        ea_ref[h, :, cols] = jnp.exp(a - as_ref[0:1, :])
        rk_ref[h, :, cols] = rank.astype(rk_ref.dtype)
        eb_ref[h, :, cols] = (jnp.exp(b - bs_ref[0:1, :]) / zsum).astype(eb_ref.dtype)

    def head(h, carry):
        ra = pl.multiple_of(2 * h * N_KEYS, N_KEYS)
        rb = pl.multiple_of((2 * h + 1) * N_KEYS, N_KEYS)
        at_ref[...] = jnp.dot(keys_ref[2 * h], qp_ref[pl.ds(ra, N_KEYS), :],
                              preferred_element_type=F32)
        bt_ref[...] = jnp.dot(keys_ref[2 * h + 1], qp_ref[pl.ds(rb, N_KEYS), :],
                              preferred_element_type=F32)

        def body(s, carry):
            strip(h, pl.multiple_of(s * LANES, LANES))
            return carry

        return lax.fori_loop(0, tm // LANES, body, carry)

    lax.fori_loop(0, nh, head, 0)


def _peer_sel(h2t, wqt, keys, tm):
    d, l = h2t.shape
    nh = keys.shape[0] // 2
    blk = pl.BlockSpec((nh, N_KEYS, tm), lambda i: (0, 0, i))
    by_a = jax.ShapeDtypeStruct((nh, N_KEYS, l), F32)
    by_b = jax.ShapeDtypeStruct((nh, N_KEYS, l), MXU_DTYPE)
    return pl.pallas_call(
        _peer_sel_kernel,
        grid=(l // tm,),
        in_specs=[pl.BlockSpec((d, tm), lambda i: (0, i)),
                  pl.BlockSpec(wqt.shape, lambda i: (0, 0)),
                  pl.BlockSpec(keys.shape, lambda i: (0, 0, 0))],
        out_specs=[blk, blk, blk, blk],
        out_shape=[by_a, by_a, by_b, by_b],
        scratch_shapes=[pltpu.VMEM((wqt.shape[0], tm), MXU_DTYPE),
                        pltpu.VMEM((N_KEYS, tm), F32), pltpu.VMEM((N_KEYS, tm), F32),
                        pltpu.VMEM((PEER_TOPK, LANES), F32), pltpu.VMEM((PEER_TOPK, LANES), F32),
                        pltpu.VMEM((N_CAND, LANES), F32)],
        compiler_params=_cparams(("arbitrary",)), name="peer_sel",
    )(h2t, wqt, keys)


def _peer_dense_kernel(ht_ref, u_ref, vt_ref, cnt_ref, ea_ref, rk_ref, eb_ref, x_ref, g2_ref, o_ref,
                       act_ref, acc_ref):
    e = pl.program_id(1)
    n_tiles = pl.num_programs(1) - 1
    nh = cnt_ref.shape[0]
    ni = u_ref.shape[0] // N_KEYS
    gdt = rk_ref.dtype
    tile_rows = 16

    def row_bcast(row):
        one = jnp.broadcast_to(row, (tile_rows, row.shape[1])).astype(gdt)
        return jnp.concatenate([one] * (N_KEYS // tile_rows), axis=0)

    @pl.when(e == 0)
    def _():
        acc_ref[...] = jnp.zeros_like(acc_ref)
        act_ref[...] = jnp.dot(u_ref[...], ht_ref[...], preferred_element_type=F32)

    @pl.when(e > 0)
    def _():
        act = _gelu(act_ref[...]).astype(gdt)
        act_ref[...] = jnp.dot(u_ref[...], ht_ref[...], preferred_element_type=F32)
        ws = []
        for il in range(ni):
            i_glob = (e - 1) * ni + il
            g = None
            for h in range(nh):
                cnt = row_bcast(cnt_ref[h, pl.ds(i_glob, 1), :])
                ea = row_bcast(ea_ref[h, pl.ds(i_glob, 1), :])
                t = jnp.where(rk_ref[h] < cnt, eb_ref[h], jnp.zeros((), gdt)) * ea
                g = t if g is None else g + t
            ws.append(g * act[il * N_KEYS:(il + 1) * N_KEYS, :])
        w = jnp.concatenate(ws, axis=0)
        acc_ref[...] += jnp.dot(vt_ref[...], w, preferred_element_type=F32)

    @pl.when(e == n_tiles)
    def _():
        o_ref[...] = x_ref[...] + g2_ref[...] * acc_ref[...].T


def _peer_dense(h2t, u, vt, cnt, ea, rk, eb, xn, g2, tt, te):
    d, l = h2t.shape
    n_tiles = u.shape[0] // te
    nh = cnt.shape[0]
    sel = pl.BlockSpec((nh, N_KEYS, tt), lambda i, e: (0, 0, i))
    return pl.pallas_call(
        _peer_dense_kernel,
        grid=(l // tt, n_tiles + 1),
        in_specs=[pl.BlockSpec((d, tt), lambda i, e: (0, i)),
                  pl.BlockSpec((te, d), lambda i, e: (jnp.minimum(e, n_tiles - 1), 0)),
                  pl.BlockSpec((d, te), lambda i, e: (0, jnp.maximum(e - 1, 0))),
                  sel, sel, sel, sel,
                  pl.BlockSpec((tt, d), lambda i, e: (i, 0)),
                  pl.BlockSpec((1, d), lambda i, e: (0, 0))],
        out_specs=pl.BlockSpec((tt, d), lambda i, e: (i, 0)),
        out_shape=jax.ShapeDtypeStruct((l, d), F32),
        scratch_shapes=[pltpu.VMEM((te, tt), F32), pltpu.VMEM((d, tt), F32)],
        compiler_params=_cparams(("arbitrary", "arbitrary")), name="peer_dense",
    )(h2t, u, vt, cnt, ea, rk, eb, xn, g2)


def _rope_tables(n_tok):
    n_rows = n_tok // GRID_W
    rows = jnp.repeat(jnp.arange(n_rows, dtype=F32), GRID_W)
    cols = jnp.tile(jnp.arange(GRID_W, dtype=F32), n_rows)
    n_freq = DIFF_QK // 4
    inv = ROPE_BASE ** (-jnp.arange(n_freq, dtype=F32) / n_freq)
    ang = jnp.concatenate([rows[:, None] * inv, cols[:, None] * inv], axis=-1)
    ang = jnp.repeat(ang, 2, axis=-1)
    sign = jnp.tile(jnp.array([-1.0, 1.0], F32), DIFF_QK // 2)
    cos = jnp.tile(jnp.cos(ang), (1, LANES // DIFF_QK))
    sin = jnp.tile(jnp.sin(ang) * sign, (1, LANES // DIFF_QK))
    return cos, sin


def kernel(x, c, ctx, c_ctx, w_ada, b_ada, norm1_g, norm2_g, w_in, gmlp_ln_g, gmlp_ws, gmlp_bs,
           q_norm_g, k_norm_g, lambda_q1, lambda_k1, lambda_q2, lambda_k2, subln_g, w_out,
           peer_wq, peer_keys, peer_u, peer_v):
    depth = w_ada.shape[0]
    assert depth == 1 and x.shape[0] == 1
    _, seq, d = x.shape
    lc = ctx.shape[1]
    da = GMLP_GROUPS * CHUNK
    dqk = DIFF_HEADS * 2 * DIFF_QK
    lam_init = 0.8 - 0.6 * math.exp(-0.3 * 0)
    row = lambda a: a.reshape(1, -1).astype(F32)

    cond8 = jnp.zeros((8, d), F32).at[0].set(c[0]).at[1].set(c_ctx)
    mod = _ada(cond8, w_ada[0], b_ada[0])
    sh1, sc1, g1, sh2, sc2, g2 = [mod[0:1, k * d:(k + 1) * d] for k in range(6)]
    sh1c, sc1c = mod[1:2, 0:d], mod[1:2, d:2 * d]

    w_in_b = w_in[0].astype(MXU_DTYPE)
    ones_bd = jnp.kron(jnp.eye(LANES // DIFF_QK, dtype=F32),
                       jnp.ones((DIFF_QK, DIFF_QK), F32)).astype(MXU_DTYPE)
    qg = jnp.tile(row(q_norm_g[0]), (1, dqk // DIFF_QK))
    kg = jnp.tile(row(k_norm_g[0]), (1, dqk // DIFF_QK))
    lng = row(gmlp_ln_g[0])
    cos, sin = _rope_tables(seq)
    q_scale = (DIFF_QK ** -0.5) * math.log2(math.e)

    z, vt = _in_proj(x[0], row(norm1_g[0]), sh1, sc1, w_in_b, lng, qg, kg, cos, sin, ones_bd,
                     ("u", "va", "q", "k", "v"), q_scale, tm=min(ROWS_IN, seq))
    zc, vct = _in_proj(ctx[0], row(norm1_g[0]), sh1c, sc1c, w_in_b[:, 2 * da + dqk:], lng, qg, kg,
                       jnp.ones((lc, LANES), F32), jnp.zeros((lc, LANES), F32), ones_bd,
                       ("k", "v"), 1.0, tm=lc)

    attn = _attention(z, vt, zc, vct, subln_g[0], lambda_q1[0], lambda_k1[0], lambda_q2[0],
                      lambda_k2[0], lam_init, tq=min(ATTN_Q, seq), tk=min(ATTN_KEYS, seq))
    xn, h2t = _out_proj(z, gmlp_ws[0].astype(MXU_DTYPE), gmlp_bs[0].T.astype(F32), attn,
                        w_out[0].astype(MXU_DTYPE), x[0], g1, row(norm2_g[0]), sh2, sc2,
                        tm=min(ROWS_OUT, seq))

    keys = peer_keys[0].reshape(PEER_HEADS * 2, N_KEYS, -1).astype(MXU_DTYPE)
    cnt, ea, rk, eb = _peer_sel(h2t, peer_wq[0].T.astype(MXU_DTYPE), keys,
                                tm=min(SEL_TOKENS, seq))
    out = _peer_dense(h2t, peer_u[0].astype(MXU_DTYPE), peer_v[0].T.astype(MXU_DTYPE),
                      cnt, ea, rk, eb, xn, g2, tt=min(DENSE_TOKENS, seq), te=DENSE_EXPERTS)
    return out[None]
```

```python
import functools
import math

import jax
import jax.numpy as jnp
from jax import lax
from jax.experimental import pallas as pl
from jax.experimental.pallas import tpu as pltpu

F32 = jnp.float32
MXU_DTYPE = jnp.bfloat16

EPS = 1e-6
GRID_W = 64
ROPE_BASE = 10000.0

GMLP_GROUPS = 8
CHUNK = 128
DIFF_HEADS = 8
DIFF_QK = 64
DIFF_V = 128
PEER_HEADS = 8
N_KEYS = 128
PEER_TOPK = 16

LANES = 128
NEG_BIG = -1e30

VMEM_LIMIT = 56 * 1024 * 1024

ADA_COLS = 1024
ROWS_IN = 512
ATTN_Q = 256
ATTN_KEYS = 1024
ROWS_OUT = 512
SEL_TOKENS = 512
DENSE_TOKENS = 512
DENSE_EXPERTS = 1024


def _cparams(sem):
    return pltpu.CompilerParams(dimension_semantics=sem, vmem_limit_bytes=VMEM_LIMIT)


def _gelu(x):
    return 0.5 * x * (1.0 + lax.erf(x * (2.0 ** -0.5)))


def _ada_kernel(c_ref, w_ref, b_ref, o_ref):
    c = c_ref[...]
    s = c * jax.nn.sigmoid(c)
    o_ref[...] = jnp.dot(s.astype(MXU_DTYPE), w_ref[...].astype(MXU_DTYPE),
                         preferred_element_type=F32) + b_ref[...]


def _ada(cond8, w_ada, b_ada):
    d, n = w_ada.shape
    tn = ADA_COLS
    return pl.pallas_call(
        _ada_kernel,
        grid=(n // tn,),
        in_specs=[pl.BlockSpec((8, d), lambda j: (0, 0)),
                  pl.BlockSpec((d, tn), lambda j: (0, j)),
                  pl.BlockSpec((1, tn), lambda j: (0, j))],
        out_specs=pl.BlockSpec((8, tn), lambda j: (0, j)),
        out_shape=jax.ShapeDtypeStruct((8, n), F32),
        compiler_params=_cparams(("arbitrary",)), name="ada",
    )(cond8, w_ada, b_ada.reshape(1, n))


def _group_sum_64(xx, ones_bd):
    outs = []
    for b in range(xx.shape[1] // LANES):
        blk = xx[:, b * LANES:(b + 1) * LANES]
        hi = blk.astype(MXU_DTYPE)
        lo = (blk - hi.astype(F32)).astype(MXU_DTYPE)
        outs.append(jnp.dot(hi, ones_bd, preferred_element_type=F32)
                    + jnp.dot(lo, ones_bd, preferred_element_type=F32))
    return jnp.concatenate(outs, axis=1)


def _swap_pairs(x):
    n = x.shape[1]
    lane = lax.broadcasted_iota(jnp.int32, x.shape, 1)
    nxt = pltpu.roll(x, n - 1, 1)
    prv = pltpu.roll(x, 1, 1)
    return jnp.where((lane & 1) == 0, nxt, prv)


def _in_proj_kernel(sections, q_scale,
                    x_ref, ng_ref, sh_ref, sc_ref, w_ref, lng_ref, qg_ref, kg_ref,
                    cos_ref, sin_ref, ones_ref, o_ref, vt_ref):
    x = x_ref[...]
    y = x * lax.rsqrt(jnp.mean(x * x, axis=-1, keepdims=True) + EPS) * ng_ref[...]
    h = (y * (1.0 + sc_ref[...]) + sh_ref[...]).astype(w_ref.dtype)
    tn = w_ref.shape[1] // len(sections)

    def qk_norm_rope(z, g_ref, scale):
        ms = _group_sum_64(z * z, ones_ref[...]) * (1.0 / DIFF_QK)
        y = z * lax.rsqrt(ms + EPS) * g_ref[...]
        reps = tn // LANES
        cos = jnp.concatenate([cos_ref[...]] * reps, axis=1)
        sin = jnp.concatenate([sin_ref[...]] * reps, axis=1)
        y = y * cos + _swap_pairs(y) * sin
        if scale != 1.0:
            y = y * scale
        return y

    for idx, kind in enumerate(sections):
        acc = jnp.dot(h, w_ref[:, idx * tn:(idx + 1) * tn], preferred_element_type=F32)
        cols = slice(idx * tn, (idx + 1) * tn)
        if kind == "u":
            o_ref[:, cols] = _gelu(acc).astype(o_ref.dtype)
        elif kind == "va":
            g = _gelu(acc)
            for b in range(tn // CHUNK):
                blk = g[:, b * CHUNK:(b + 1) * CHUNK]
                mu = jnp.mean(blk, axis=-1, keepdims=True)
                d = blk - mu
                y = d * lax.rsqrt(jnp.mean(d * d, axis=-1, keepdims=True) + EPS)
                y = y * lng_ref[:, b * CHUNK:(b + 1) * CHUNK]
                o_ref[:, idx * tn + b * CHUNK:idx * tn + (b + 1) * CHUNK] = y.astype(o_ref.dtype)
        elif kind == "q":
            o_ref[:, cols] = qk_norm_rope(acc, qg_ref, q_scale).astype(o_ref.dtype)
        elif kind == "k":
            o_ref[:, cols] = qk_norm_rope(acc, kg_ref, 1.0).astype(o_ref.dtype)
        else:
            vt_ref[...] = acc.T.astype(vt_ref.dtype)


def _in_proj(x2, ng, sh, sc, w, lng, qg, kg, cos, sin, ones_bd, sections, q_scale, tm):
    l, d = x2.shape
    n = w.shape[1]
    tn = n // len(sections)
    assert sections[-1] == "v"
    cst = lambda i: (0, 0)
    return pl.pallas_call(
        functools.partial(_in_proj_kernel, sections, q_scale),
        grid=(l // tm,),
        in_specs=[pl.BlockSpec((tm, d), lambda i: (i, 0)),
                  pl.BlockSpec((1, d), cst), pl.BlockSpec((1, d), cst), pl.BlockSpec((1, d), cst),
                  pl.BlockSpec((d, n), cst, pipeline_mode=pl.Buffered(1)),
                  pl.BlockSpec((1, tn), cst), pl.BlockSpec((1, tn), cst), pl.BlockSpec((1, tn), cst),
                  pl.BlockSpec((tm, LANES), lambda i: (i, 0)),
                  pl.BlockSpec((tm, LANES), lambda i: (i, 0)),
                  pl.BlockSpec((LANES, LANES), cst)],
        out_specs=[pl.BlockSpec((tm, n - tn), lambda i: (i, 0)),
                   pl.BlockSpec((tn, tm), lambda i: (0, i))],
        out_shape=[jax.ShapeDtypeStruct((l, n - tn), MXU_DTYPE),
                   jax.ShapeDtypeStruct((tn, l), MXU_DTYPE)],
        compiler_params=_cparams(("arbitrary",)), name="in_proj",
    )(x2, ng, sh, sc, w, lng, qg, kg, cos, sin, ones_bd)


def _attn_kernel(tk, lam_init,
                 q_ref, k_ref, vt_ref, kc_ref, vct_ref, g_ref, l1_ref, l2_ref, l3_ref, l4_ref,
                 o_ref, sa_ref, sb_ref, pa_ref, pb_ref):
    tq = q_ref.shape[0]
    qt = q_ref[...].astype(F32).T.astype(MXU_DTYPE)
    sub = lax.broadcasted_iota(jnp.int32, qt.shape, 0)
    zero = jnp.zeros_like(qt)
    qbd = jnp.concatenate([jnp.where(sub < DIFF_QK, qt, zero),
                           jnp.where(sub >= DIFF_QK, qt, zero)], axis=1)
    n_chunks = k_ref.shape[0] // tk

    def scores(c, s_ref):
        s_ref[...] = jnp.dot(k_ref[c * tk:(c + 1) * tk, :], qbd, preferred_element_type=F32)

    def softmax(s_ref, p_ref, state):
        m, l, _, acc = state
        s = s_ref[...]
        m_new = jnp.maximum(m, jnp.max(s, axis=0, keepdims=True))
        p = jnp.exp2(s - m_new)
        alpha = jnp.exp2(m - m_new)
        l = alpha * l + jnp.sum(p, axis=0, keepdims=True)
        p_ref[...] = p.astype(p_ref.dtype)
        return m_new, l, alpha, acc

    def values(c, p_ref, state):
        m, l, alpha, acc = state
        vt = vt_ref[:, c * tk:(c + 1) * tk]
        return m, l, alpha, alpha * acc + jnp.dot(vt, p_ref[...], preferred_element_type=F32)

    s = jnp.dot(kc_ref[...], qbd, preferred_element_type=F32)
    m = jnp.max(s, axis=0, keepdims=True)
    p = jnp.exp2(s - m)
    state = (m, jnp.sum(p, axis=0, keepdims=True), jnp.ones_like(m),
             jnp.dot(vct_ref[...], p.astype(MXU_DTYPE), preferred_element_type=F32))

    s_bufs, p_bufs = (sa_ref, sb_ref), (pa_ref, pb_ref)
    scores(0, s_bufs[0])
    for c in range(n_chunks):
        if c + 1 < n_chunks:
            scores(c + 1, s_bufs[(c + 1) % 2])
        state = softmax(s_bufs[c % 2], p_bufs[c % 2], state)
        state = values(c, p_bufs[c % 2], state)
    _, l, _, acc = state

    lam = (jnp.exp(jnp.sum(l1_ref[...] * l2_ref[...], axis=-1, keepdims=True))
           - jnp.exp(jnp.sum(l3_ref[...] * l4_ref[...], axis=-1, keepdims=True)) + lam_init)
    o = acc[:, :tq] / l[:, :tq] - lam * (acc[:, tq:] / l[:, tq:])
    y = o * lax.rsqrt(jnp.mean(o * o, axis=0, keepdims=True) + EPS) * g_ref[...]
    o_ref[...] = (y * (1.0 - lam_init)).T.astype(o_ref.dtype)


def _attention(z, vt, zc, vct, subln_g, lq1, lk1, lq2, lk2, lam_init, tq, tk):
    l = z.shape[0]
    lc = zc.shape[0]
    nh = DIFF_HEADS
    qb, kb = 2 * nh, 3 * nh
    vec = lambda a: a.reshape(1, -1).astype(F32)
    cst = lambda h, i: (0, 0)
    return pl.pallas_call(
        functools.partial(_attn_kernel, tk, lam_init),
        grid=(nh, l // tq),
        in_specs=[pl.BlockSpec((tq, LANES), lambda h, i: (i, qb + h)),
                  pl.BlockSpec((l, LANES), lambda h, i: (0, kb + h)),
                  pl.BlockSpec((DIFF_V, l), lambda h, i: (h, 0)),
                  pl.BlockSpec((lc, LANES), lambda h, i: (0, h)),
                  pl.BlockSpec((DIFF_V, lc), lambda h, i: (h, 0)),
                  pl.BlockSpec((DIFF_V, 1), cst),
                  pl.BlockSpec((1, DIFF_QK), cst), pl.BlockSpec((1, DIFF_QK), cst),
                  pl.BlockSpec((1, DIFF_QK), cst), pl.BlockSpec((1, DIFF_QK), cst)],
        out_specs=pl.BlockSpec((tq, DIFF_V), lambda h, i: (i, h)),
        out_shape=jax.ShapeDtypeStruct((l, nh * DIFF_V), MXU_DTYPE),
        scratch_shapes=[pltpu.VMEM((tk, 2 * tq), F32), pltpu.VMEM((tk, 2 * tq), F32),
                        pltpu.VMEM((tk, 2 * tq), MXU_DTYPE), pltpu.VMEM((tk, 2 * tq), MXU_DTYPE)],
        compiler_params=_cparams(("arbitrary", "arbitrary")), name="attention",
    )(z, z, vt, zc, vct, subln_g.reshape(-1, 1).astype(F32), vec(lq1), vec(lk1), vec(lq2), vec(lk2))


def _out_proj_kernel(u_ref, v_ref, ws_ref, bs_ref, t_ref, w_ref, x_ref, g1_ref, ng_ref, sh_ref,
                     sc_ref, xn_ref, h2_ref):
    tm, da = u_ref.shape
    chunks = []
    for c in range(tm // CHUNK):
        rows = slice(c * CHUNK, (c + 1) * CHUNK)
        groups = []
        for g in range(GMLP_GROUPS):
            cols = slice(g * CHUNK, (g + 1) * CHUNK)
            mixed = jnp.dot(ws_ref[g], v_ref[rows, cols], preferred_element_type=F32)
            mixed = mixed + bs_ref[:, g:g + 1]
            groups.append((u_ref[rows, cols].astype(F32) * mixed).astype(t_ref.dtype))
        chunks.append(jnp.concatenate(groups, axis=1))
    a_out = jnp.concatenate(chunks, axis=0)
    mix = (jnp.dot(a_out, w_ref[0:da, :], preferred_element_type=F32)
           + jnp.dot(t_ref[...], w_ref[da:, :], preferred_element_type=F32))
    xn = x_ref[...] + g1_ref[...] * mix
    xn_ref[...] = xn
    y = xn * lax.rsqrt(jnp.mean(xn * xn, axis=-1, keepdims=True) + EPS) * ng_ref[...]
    h2_ref[...] = (y * (1.0 + sc_ref[...]) + sh_ref[...]).T.astype(h2_ref.dtype)


def _out_proj(z, ws, bs_t, attn, w_out, x2, g1, ng, sh, sc, tm):
    l, d = x2.shape
    da, dv = GMLP_GROUPS * CHUNK, attn.shape[1]
    cst = lambda i: (0, 0)
    return pl.pallas_call(
        _out_proj_kernel,
        grid=(l // tm,),
        in_specs=[pl.BlockSpec((tm, da), lambda i: (i, 0)),
                  pl.BlockSpec((tm, da), lambda i: (i, 1)),
                  pl.BlockSpec((GMLP_GROUPS, CHUNK, CHUNK), lambda i: (0, 0, 0)),
                  pl.BlockSpec((CHUNK, GMLP_GROUPS), cst),
                  pl.BlockSpec((tm, dv), lambda i: (i, 0)),
                  pl.BlockSpec((da + dv, d), cst),
                  pl.BlockSpec((tm, d), lambda i: (i, 0)),
                  pl.BlockSpec((1, d), cst), pl.BlockSpec((1, d), cst),
                  pl.BlockSpec((1, d), cst), pl.BlockSpec((1, d), cst)],
        out_specs=[pl.BlockSpec((tm, d), lambda i: (i, 0)),
                   pl.BlockSpec((d, tm), lambda i: (0, i))],
        out_shape=[jax.ShapeDtypeStruct((l, d), F32), jax.ShapeDtypeStruct((d, l), MXU_DTYPE)],
        compiler_params=_cparams(("arbitrary",)), name="out_proj",
    )(z, z, ws, bs_t, attn, w_out, x2, g1, ng, sh, sc)


SUBL = 8
N_CAND = 10 * SUBL

_SORT16 = ((0, 1), (2, 3), (0, 2), (1, 3), (1, 2), (4, 5), (6, 7), (4, 6), (5, 7), (5, 6), (0, 4),
           (2, 6), (2, 4), (1, 5), (3, 7), (3, 5), (1, 2), (3, 4), (5, 6), (8, 9), (10, 11), (8, 10),
           (9, 11), (9, 10), (12, 13), (14, 15), (12, 14), (13, 15), (13, 14), (8, 12), (10, 14),
           (10, 12), (9, 13), (11, 15), (11, 13), (9, 10), (11, 12), (13, 14), (0, 8), (4, 12),
           (4, 8), (2, 10), (6, 14), (6, 10), (2, 4), (6, 8), (10, 12), (1, 9), (5, 13), (5, 9),
           (3, 11), (7, 15), (7, 11), (3, 5), (7, 9), (11, 13), (1, 2), (3, 4), (5, 6), (7, 8),
           (9, 10), (11, 12), (13, 14))


def _peer_sel_kernel(ht_ref, wqt_ref, keys_ref, cnt_ref, ea_ref, rk_ref, eb_ref,
                     qp_ref, at_ref, bt_ref, as_ref, bs_ref, cand_ref):
    tm = ht_ref.shape[1]
    qp_ref[...] = jnp.dot(wqt_ref[...], ht_ref[...],
                          preferred_element_type=F32).astype(qp_ref.dtype)
    nh = cnt_ref.shape[0]

    def top16(s, out_ref):
        v = [s[k * SUBL:(k + 1) * SUBL, :] for k in range(N_KEYS // SUBL)]
        for i, j in _SORT16:
            v[i], v[j] = jnp.maximum(v[i], v[j]), jnp.minimum(v[i], v[j])
        for r in range(PEER_TOPK):
            m = jnp.max(v[0], axis=0, keepdims=True)
            out_ref[r:r + 1, :] = m
            hit = v[0] == m
            for k in range(PEER_TOPK - 1 - r):
                v[k] = jnp.where(hit, v[k + 1], v[k])

    def strip(h, c0):
        cols = pl.ds(c0, LANES)
        a = at_ref[:, cols]
        b = bt_ref[:, cols]
        top16(a, as_ref)
        top16(b, bs_ref)
        cand_ref[0:PEER_TOPK, :] = as_ref[0:1, :] + bs_ref[...]
        for i in range(1, SUBL):
            r0 = PEER_TOPK + SUBL * (i - 1)
            cand_ref[r0:r0 + SUBL, :] = as_ref[i:i + 1, :] + bs_ref[0:SUBL, :]
        cand_ref[N_CAND - SUBL:N_CAND, :] = as_ref[SUBL:PEER_TOPK, :] + bs_ref[0:1, :]
        c = cand_ref[...]
        cmax = jnp.max(c, axis=0, keepdims=True)
        m = cmax
        zsum = jnp.ones_like(cmax)
        for r in range(1, PEER_TOPK):
            c = jnp.where(c == m, NEG_BIG, c)
            m = jnp.max(c, axis=0, keepdims=True)
            zsum = zsum + jnp.exp(m - cmax)
        tau = m
        bs = bs_ref[...]
        cnt = jnp.zeros(a.shape, F32)
        rank = jnp.full(b.shape, float(PEER_TOPK), F32)
        for r in range(PEER_TOPK):
            ar = as_ref[r:r + 1, :]
            n_r = jnp.sum(jnp.where(ar + bs >= tau, 1.0, 0.0), axis=0, keepdims=True)
            cnt = jnp.where(a == ar, n_r, cnt)
            rank = jnp.where(b == bs_ref[r:r + 1, :], float(r), rank)
        cnt_ref[h, :, cols] = cnt
        ea_ref[h, :, cols] = jnp.exp(a - as_ref[0:1, :])
        rk_ref[h, :, cols] = rank.astype(rk_ref.dtype)
        eb_ref[h, :, cols] = (jnp.exp(b - bs_ref[0:1, :]) / zsum).astype(eb_ref.dtype)

    def head(h, carry):
        ra = pl.multiple_of(2 * h * N_KEYS, N_KEYS)
        rb = pl.multiple_of((2 * h + 1) * N_KEYS, N_KEYS)
        at_ref[...] = jnp.dot(keys_ref[2 * h], qp_ref[pl.ds(ra, N_KEYS), :],
                              preferred_element_type=F32)
        bt_ref[...] = jnp.dot(keys_ref[2 * h + 1], qp_ref[pl.ds(rb, N_KEYS), :],
                              preferred_element_type=F32)

        def body(s, carry):
            strip(h, pl.multiple_of(s * LANES, LANES))
            return carry

        return lax.fori_loop(0, tm // LANES, body, carry)

    lax.fori_loop(0, nh, head, 0)


def _peer_sel(h2t, wqt, keys, tm):
    d, l = h2t.shape
    nh = keys.shape[0] // 2
    blk = pl.BlockSpec((nh, N_KEYS, tm), lambda i: (0, 0, i))
    by_a = jax.ShapeDtypeStruct((nh, N_KEYS, l), F32)
    by_b = jax.ShapeDtypeStruct((nh, N_KEYS, l), MXU_DTYPE)
    return pl.pallas_call(
        _peer_sel_kernel,
        grid=(l // tm,),
        in_specs=[pl.BlockSpec((d, tm), lambda i: (0, i)),
                  pl.BlockSpec(wqt.shape, lambda i: (0, 0)),
                  pl.BlockSpec(keys.shape, lambda i: (0, 0, 0))],
        out_specs=[blk, blk, blk, blk],
        out_shape=[by_a, by_a, by_b, by_b],
        scratch_shapes=[pltpu.VMEM((wqt.shape[0], tm), MXU_DTYPE),
                        pltpu.VMEM((N_KEYS, tm), F32), pltpu.VMEM((N_KEYS, tm), F32),
                        pltpu.VMEM((PEER_TOPK, LANES), F32), pltpu.VMEM((PEER_TOPK, LANES), F32),
                        pltpu.VMEM((N_CAND, LANES), F32)],
        compiler_params=_cparams(("arbitrary",)), name="peer_sel",
    )(h2t, wqt, keys)


def _peer_dense_kernel(ht_ref, u_ref, vt_ref, cnt_ref, ea_ref, rk_ref, eb_ref, x_ref, g2_ref, o_ref,
                       act_ref, acc_ref):
    e = pl.program_id(1)
    n_tiles = pl.num_programs(1) - 1
    nh = cnt_ref.shape[0]
    ni = u_ref.shape[0] // N_KEYS
    gdt = rk_ref.dtype
    tile_rows = 16

    def row_bcast(row):
        one = jnp.broadcast_to(row, (tile_rows, row.shape[1])).astype(gdt)
        return jnp.concatenate([one] * (N_KEYS // tile_rows), axis=0)

    @pl.when(e == 0)
    def _():
        acc_ref[...] = jnp.zeros_like(acc_ref)
        act_ref[...] = jnp.dot(u_ref[...], ht_ref[...], preferred_element_type=F32)

    @pl.when(e > 0)
    def _():
        act = _gelu(act_ref[...]).astype(gdt)
        act_ref[...] = jnp.dot(u_ref[...], ht_ref[...], preferred_element_type=F32)
        ws = []
        for il in range(ni):
            i_glob = (e - 1) * ni + il
            g = None
            for h in range(nh):
                cnt = row_bcast(cnt_ref[h, pl.ds(i_glob, 1), :])
                ea = row_bcast(ea_ref[h, pl.ds(i_glob, 1), :])
                t = jnp.where(rk_ref[h] < cnt, eb_ref[h], jnp.zeros((), gdt)) * ea
                g = t if g is None else g + t
            ws.append(g * act[il * N_KEYS:(il + 1) * N_KEYS, :])
        w = jnp.concatenate(ws, axis=0)
        acc_ref[...] += jnp.dot(vt_ref[...], w, preferred_element_type=F32)

    @pl.when(e == n_tiles)
    def _():
        o_ref[...] = x_ref[...] + g2_ref[...] * acc_ref[...].T


def _peer_dense(h2t, u, vt, cnt, ea, rk, eb, xn, g2, tt, te):
    d, l = h2t.shape
    n_tiles = u.shape[0] // te
    nh = cnt.shape[0]
    sel = pl.BlockSpec((nh, N_KEYS, tt), lambda i, e: (0, 0, i))
    return pl.pallas_call(
        _peer_dense_kernel,
        grid=(l // tt, n_tiles + 1),
        in_specs=[pl.BlockSpec((d, tt), lambda i, e: (0, i)),
                  pl.BlockSpec((te, d), lambda i, e: (jnp.minimum(e, n_tiles - 1), 0)),
                  pl.BlockSpec((d, te), lambda i, e: (0, jnp.maximum(e - 1, 0))),
                  sel, sel, sel, sel,
                  pl.BlockSpec((tt, d), lambda i, e: (i, 0), pipeline_mode=pl.Buffered(1)),
                  pl.BlockSpec((1, d), lambda i, e: (0, 0))],
        out_specs=pl.BlockSpec((tt, d), lambda i, e: (i, 0)),
        out_shape=jax.ShapeDtypeStruct((l, d), F32),
        scratch_shapes=[pltpu.VMEM((te, tt), F32), pltpu.VMEM((d, tt), F32)],
        compiler_params=_cparams(("arbitrary", "arbitrary")), name="peer_dense",
    )(h2t, u, vt, cnt, ea, rk, eb, xn, g2)


def _rope_tables(n_tok):
    n_rows = n_tok // GRID_W
    rows = jnp.repeat(jnp.arange(n_rows, dtype=F32), GRID_W)
    cols = jnp.tile(jnp.arange(GRID_W, dtype=F32), n_rows)
    n_freq = DIFF_QK // 4
    inv = ROPE_BASE ** (-jnp.arange(n_freq, dtype=F32) / n_freq)
    ang = jnp.concatenate([rows[:, None] * inv, cols[:, None] * inv], axis=-1)
    ang = jnp.repeat(ang, 2, axis=-1)
    sign = jnp.tile(jnp.array([-1.0, 1.0], F32), DIFF_QK // 2)
    cos = jnp.tile(jnp.cos(ang), (1, LANES // DIFF_QK))
    sin = jnp.tile(jnp.sin(ang) * sign, (1, LANES // DIFF_QK))
    return cos, sin


def kernel(x, c, ctx, c_ctx, w_ada, b_ada, norm1_g, norm2_g, w_in, gmlp_ln_g, gmlp_ws, gmlp_bs,
           q_norm_g, k_norm_g, lambda_q1, lambda_k1, lambda_q2, lambda_k2, subln_g, w_out,
           peer_wq, peer_keys, peer_u, peer_v):
    depth = w_ada.shape[0]
    assert depth == 1 and x.shape[0] == 1
    _, seq, d = x.shape
    lc = ctx.shape[1]
    da = GMLP_GROUPS * CHUNK
    dqk = DIFF_HEADS * 2 * DIFF_QK
    lam_init = 0.8 - 0.6 * math.exp(-0.3 * 0)
    row = lambda a: a.reshape(1, -1).astype(F32)

    cond8 = jnp.zeros((8, d), F32).at[0].set(c[0]).at[1].set(c_ctx)
    mod = _ada(cond8, w_ada[0], b_ada[0])
    sh1, sc1, g1, sh2, sc2, g2 = [mod[0:1, k * d:(k + 1) * d] for k in range(6)]
    sh1c, sc1c = mod[1:2, 0:d], mod[1:2, d:2 * d]

    w_in_b = w_in[0].astype(MXU_DTYPE)
    ones_bd = jnp.kron(jnp.eye(LANES // DIFF_QK, dtype=F32),
                       jnp.ones((DIFF_QK, DIFF_QK), F32)).astype(MXU_DTYPE)
    qg = jnp.tile(row(q_norm_g[0]), (1, dqk // DIFF_QK))
    kg = jnp.tile(row(k_norm_g[0]), (1, dqk // DIFF_QK))
    lng = row(gmlp_ln_g[0])
    cos, sin = _rope_tables(seq)
    q_scale = (DIFF_QK ** -0.5) * math.log2(math.e)

    z, vt = _in_proj(x[0], row(norm1_g[0]), sh1, sc1, w_in_b, lng, qg, kg, cos, sin, ones_bd,
                     ("u", "va", "q", "k", "v"), q_scale, tm=min(ROWS_IN, seq))
    zc, vct = _in_proj(ctx[0], row(norm1_g[0]), sh1c, sc1c, w_in_b[:, 2 * da + dqk:], lng, qg, kg,
                       jnp.ones((lc, LANES), F32), jnp.zeros((lc, LANES), F32), ones_bd,
                       ("k", "v"), 1.0, tm=lc)

    attn = _attention(z, vt, zc, vct, subln_g[0], lambda_q1[0], lambda_k1[0], lambda_q2[0],
                      lambda_k2[0], lam_init, tq=min(ATTN_Q, seq), tk=min(ATTN_KEYS, seq))
    xn, h2t = _out_proj(z, gmlp_ws[0].astype(MXU_DTYPE), gmlp_bs[0].T.astype(F32), attn,
                        w_out[0].astype(MXU_DTYPE), x[0], g1, row(norm2_g[0]), sh2, sc2,
                        tm=min(ROWS_OUT, seq))

    keys = peer_keys[0].reshape(PEER_HEADS * 2, N_KEYS, -1).astype(MXU_DTYPE)
    cnt, ea, rk, eb = _peer_sel(h2t, peer_wq[0].T.astype(MXU_DTYPE), keys,
                                tm=min(SEL_TOKENS, seq))
    out = _peer_dense(h2t, peer_u[0].astype(MXU_DTYPE), peer_v[0].T.astype(MXU_DTYPE),
                      cnt, ea, rk, eb, xn, g2, tt=min(DENSE_TOKENS, seq), te=DENSE_EXPERTS)
    return out[None]
```

```python
import functools
import math

import jax
import jax.numpy as jnp
from jax import lax
from jax.experimental import pallas as pl
from jax.experimental.pallas import tpu as pltpu

F32 = jnp.float32
MXU_DTYPE = jnp.bfloat16

EPS = 1e-6
GRID_W = 64
ROPE_BASE = 10000.0

GMLP_GROUPS = 8
CHUNK = 128
DIFF_HEADS = 8
DIFF_QK = 64
DIFF_V = 128
PEER_HEADS = 8
N_KEYS = 128
PEER_TOPK = 16

LANES = 128
NEG_BIG = -1e30

VMEM_LIMIT = 56 * 1024 * 1024

ADA_COLS = 1024
ROWS_IN = 512
ATTN_Q = 256
ATTN_UNITS = 2
ATTN_KEYS = 1024
ROWS_OUT = 512
SEL_TOKENS = 512
DENSE_TOKENS = 512
DENSE_EXPERTS = 1024


def _cparams(sem):
    return pltpu.CompilerParams(dimension_semantics=sem, vmem_limit_bytes=VMEM_LIMIT)


def _gelu(x):
    return 0.5 * x * (1.0 + lax.erf(x * (2.0 ** -0.5)))


def _ada_kernel(c_ref, w_ref, b_ref, o_ref):
    c = c_ref[...]
    s = c * jax.nn.sigmoid(c)
    o_ref[...] = jnp.dot(s.astype(MXU_DTYPE), w_ref[...].astype(MXU_DTYPE),
                         preferred_element_type=F32) + b_ref[...]


def _ada(cond8, w_ada, b_ada):
    d, n = w_ada.shape
    tn = ADA_COLS
    return pl.pallas_call(
        _ada_kernel,
        grid=(n // tn,),
        in_specs=[pl.BlockSpec((8, d), lambda j: (0, 0)),
                  pl.BlockSpec((d, tn), lambda j: (0, j)),
                  pl.BlockSpec((1, tn), lambda j: (0, j))],
        out_specs=pl.BlockSpec((8, tn), lambda j: (0, j)),
        out_shape=jax.ShapeDtypeStruct((8, n), F32),
        compiler_params=_cparams(("arbitrary",)), name="ada",
    )(cond8, w_ada, b_ada.reshape(1, n))


def _group_sum_64(xx, ones_bd):
    outs = []
    for b in range(xx.shape[1] // LANES):
        blk = xx[:, b * LANES:(b + 1) * LANES]
        hi = blk.astype(MXU_DTYPE)
        lo = (blk - hi.astype(F32)).astype(MXU_DTYPE)
        outs.append(jnp.dot(hi, ones_bd, preferred_element_type=F32)
                    + jnp.dot(lo, ones_bd, preferred_element_type=F32))
    return jnp.concatenate(outs, axis=1)


def _swap_pairs(x):
    n = x.shape[1]
    lane = lax.broadcasted_iota(jnp.int32, x.shape, 1)
    nxt = pltpu.roll(x, n - 1, 1)
    prv = pltpu.roll(x, 1, 1)
    return jnp.where((lane & 1) == 0, nxt, prv)


def _in_proj_kernel(sections, q_scale,
                    x_ref, ng_ref, sh_ref, sc_ref, w_ref, lng_ref, qg_ref, kg_ref,
                    cos_ref, sin_ref, ones_ref, o_ref, vt_ref):
    x = x_ref[...]
    y = x * lax.rsqrt(jnp.mean(x * x, axis=-1, keepdims=True) + EPS) * ng_ref[...]
    h = (y * (1.0 + sc_ref[...]) + sh_ref[...]).astype(w_ref.dtype)
    tn = w_ref.shape[1] // len(sections)

    def qk_norm_rope(z, g_ref, scale):
        ms = _group_sum_64(z * z, ones_ref[...]) * (1.0 / DIFF_QK)
        y = z * lax.rsqrt(ms + EPS) * g_ref[...]
        reps = tn // LANES
        cos = jnp.concatenate([cos_ref[...]] * reps, axis=1)
        sin = jnp.concatenate([sin_ref[...]] * reps, axis=1)
        y = y * cos + _swap_pairs(y) * sin
        if scale != 1.0:
            y = y * scale
        return y

    for idx, kind in enumerate(sections):
        acc = jnp.dot(h, w_ref[:, idx * tn:(idx + 1) * tn], preferred_element_type=F32)
        cols = slice(idx * tn, (idx + 1) * tn)
        if kind == "u":
            o_ref[:, cols] = _gelu(acc).astype(o_ref.dtype)
        elif kind == "va":
            g = _gelu(acc)
            for b in range(tn // CHUNK):
                blk = g[:, b * CHUNK:(b + 1) * CHUNK]
                mu = jnp.mean(blk, axis=-1, keepdims=True)
                d = blk - mu
                y = d * lax.rsqrt(jnp.mean(d * d, axis=-1, keepdims=True) + EPS)
                y = y * lng_ref[:, b * CHUNK:(b + 1) * CHUNK]
                o_ref[:, idx * tn + b * CHUNK:idx * tn + (b + 1) * CHUNK] = y.astype(o_ref.dtype)
        elif kind == "q":
            o_ref[:, cols] = qk_norm_rope(acc, qg_ref, q_scale).astype(o_ref.dtype)
        elif kind == "k":
            o_ref[:, cols] = qk_norm_rope(acc, kg_ref, 1.0).astype(o_ref.dtype)
        else:
            vt_ref[...] = acc.T.astype(vt_ref.dtype)


def _in_proj(x2, ng, sh, sc, w, lng, qg, kg, cos, sin, ones_bd, sections, q_scale, tm):
    l, d = x2.shape
    n = w.shape[1]
    tn = n // len(sections)
    assert sections[-1] == "v"
    cst = lambda i: (0, 0)
    return pl.pallas_call(
        functools.partial(_in_proj_kernel, sections, q_scale),
        grid=(l // tm,),
        in_specs=[pl.BlockSpec((tm, d), lambda i: (i, 0)),
                  pl.BlockSpec((1, d), cst), pl.BlockSpec((1, d), cst), pl.BlockSpec((1, d), cst),
                  pl.BlockSpec((d, n), cst, pipeline_mode=pl.Buffered(1)),
                  pl.BlockSpec((1, tn), cst), pl.BlockSpec((1, tn), cst), pl.BlockSpec((1, tn), cst),
                  pl.BlockSpec((tm, LANES), lambda i: (i, 0)),
                  pl.BlockSpec((tm, LANES), lambda i: (i, 0)),
                  pl.BlockSpec((LANES, LANES), cst)],
        out_specs=[pl.BlockSpec((tm, n - tn), lambda i: (i, 0)),
                   pl.BlockSpec((tn, tm), lambda i: (0, i))],
        out_shape=[jax.ShapeDtypeStruct((l, n - tn), MXU_DTYPE),
                   jax.ShapeDtypeStruct((tn, l), MXU_DTYPE)],
        compiler_params=_cparams(("arbitrary",)), name="in_proj",
    )(x2, ng, sh, sc, w, lng, qg, kg, cos, sin, ones_bd)


def _attn_kernel(tk, lam_init,
                 q_ref, k_ref, vt_ref, kc_ref, vct_ref, g_ref, l1_ref, l2_ref, l3_ref, l4_ref,
                 o_ref, sa_ref, sb_ref, pa_ref, pb_ref):
    tq = q_ref.shape[0] // ATTN_UNITS
    for unit in range(ATTN_UNITS):
        rows = slice(unit * tq, (unit + 1) * tq)
        _attn_tile(tk, lam_init, q_ref.at[rows, :], k_ref, vt_ref, kc_ref, vct_ref, g_ref,
                   l1_ref, l2_ref, l3_ref, l4_ref, o_ref.at[rows, :],
                   sa_ref.at[unit], sb_ref.at[unit], pa_ref.at[unit], pb_ref.at[unit])


def _attn_tile(tk, lam_init,
               q_ref, k_ref, vt_ref, kc_ref, vct_ref, g_ref, l1_ref, l2_ref, l3_ref, l4_ref,
               o_ref, sa_ref, sb_ref, pa_ref, pb_ref):
    tq = q_ref.shape[0]
    qt = q_ref[...].astype(F32).T.astype(MXU_DTYPE)
    sub = lax.broadcasted_iota(jnp.int32, qt.shape, 0)
    zero = jnp.zeros_like(qt)
    qbd = jnp.concatenate([jnp.where(sub < DIFF_QK, qt, zero),
                           jnp.where(sub >= DIFF_QK, qt, zero)], axis=1)
    n_chunks = k_ref.shape[0] // tk

    def scores(c, s_ref):
        s_ref[...] = jnp.dot(k_ref[c * tk:(c + 1) * tk, :], qbd, preferred_element_type=F32)

    def softmax(s_ref, p_ref, state):
        m, l, _, acc = state
        s = s_ref[...]
        m_new = jnp.maximum(m, jnp.max(s, axis=0, keepdims=True))
        p = jnp.exp2(s - m_new)
        alpha = jnp.exp2(m - m_new)
        l = alpha * l + jnp.sum(p, axis=0, keepdims=True)
        p_ref[...] = p.astype(p_ref.dtype)
        return m_new, l, alpha, acc

    def values(c, p_ref, state):
        m, l, alpha, acc = state
        vt = vt_ref[:, c * tk:(c + 1) * tk]
        return m, l, alpha, alpha * acc + jnp.dot(vt, p_ref[...], preferred_element_type=F32)

    s = jnp.dot(kc_ref[...], qbd, preferred_element_type=F32)
    m = jnp.max(s, axis=0, keepdims=True)
    p = jnp.exp2(s - m)
    state = (m, jnp.sum(p, axis=0, keepdims=True), jnp.ones_like(m),
             jnp.dot(vct_ref[...], p.astype(MXU_DTYPE), preferred_element_type=F32))

    s_bufs, p_bufs = (sa_ref, sb_ref), (pa_ref, pb_ref)
    scores(0, s_bufs[0])
    for c in range(n_chunks):
        if c + 1 < n_chunks:
            scores(c + 1, s_bufs[(c + 1) % 2])
        state = softmax(s_bufs[c % 2], p_bufs[c % 2], state)
        state = values(c, p_bufs[c % 2], state)
    _, l, _, acc = state

    lam = (jnp.exp(jnp.sum(l1_ref[...] * l2_ref[...], axis=-1, keepdims=True))
           - jnp.exp(jnp.sum(l3_ref[...] * l4_ref[...], axis=-1, keepdims=True)) + lam_init)
    o = acc[:, :tq] / l[:, :tq] - lam * (acc[:, tq:] / l[:, tq:])
    y = o * lax.rsqrt(jnp.mean(o * o, axis=0, keepdims=True) + EPS) * g_ref[...]
    o_ref[...] = (y * (1.0 - lam_init)).T.astype(o_ref.dtype)


def _attention(z, vt, zc, vct, subln_g, lq1, lk1, lq2, lk2, lam_init, tq, tk):
    l = z.shape[0]
    lc = zc.shape[0]
    nh = DIFF_HEADS
    qb, kb = 2 * nh, 3 * nh
    vec = lambda a: a.reshape(1, -1).astype(F32)
    cst = lambda h, i: (0, 0)
    units = ATTN_UNITS if l % (ATTN_UNITS * tq) == 0 else 1
    assert units == ATTN_UNITS
    tq_step = units * tq
    return pl.pallas_call(
        functools.partial(_attn_kernel, tk, lam_init),
        grid=(nh, l // tq_step),
        in_specs=[pl.BlockSpec((tq_step, LANES), lambda h, i: (i, qb + h)),
                  pl.BlockSpec((l, LANES), lambda h, i: (0, kb + h)),
                  pl.BlockSpec((DIFF_V, l), lambda h, i: (h, 0)),
                  pl.BlockSpec((lc, LANES), lambda h, i: (0, h)),
                  pl.BlockSpec((DIFF_V, lc), lambda h, i: (h, 0)),
                  pl.BlockSpec((DIFF_V, 1), cst),
                  pl.BlockSpec((1, DIFF_QK), cst), pl.BlockSpec((1, DIFF_QK), cst),
                  pl.BlockSpec((1, DIFF_QK), cst), pl.BlockSpec((1, DIFF_QK), cst)],
        out_specs=pl.BlockSpec((tq_step, DIFF_V), lambda h, i: (i, h)),
        out_shape=jax.ShapeDtypeStruct((l, nh * DIFF_V), MXU_DTYPE),
        scratch_shapes=[pltpu.VMEM((units, tk, 2 * tq), F32), pltpu.VMEM((units, tk, 2 * tq), F32),
                        pltpu.VMEM((units, tk, 2 * tq), MXU_DTYPE),
                        pltpu.VMEM((units, tk, 2 * tq), MXU_DTYPE)],
        compiler_params=_cparams(("arbitrary", "arbitrary")), name="attention",
    )(z, z, vt, zc, vct, subln_g.reshape(-1, 1).astype(F32), vec(lq1), vec(lk1), vec(lq2), vec(lk2))


def _out_proj_kernel(u_ref, v_ref, ws_ref, bs_ref, t_ref, w_ref, x_ref, g1_ref, ng_ref, sh_ref,
                     sc_ref, xn_ref, h2_ref):
    tm, da = u_ref.shape
    chunks = []
    for c in range(tm // CHUNK):
        rows = slice(c * CHUNK, (c + 1) * CHUNK)
        groups = []
        for g in range(GMLP_GROUPS):
            cols = slice(g * CHUNK, (g + 1) * CHUNK)
            mixed = jnp.dot(ws_ref[g], v_ref[rows, cols], preferred_element_type=F32)
            mixed = mixed + bs_ref[:, g:g + 1]
            groups.append((u_ref[rows, cols].astype(F32) * mixed).astype(t_ref.dtype))
        chunks.append(jnp.concatenate(groups, axis=1))
    a_out = jnp.concatenate(chunks, axis=0)
    mix = (jnp.dot(a_out, w_ref[0:da, :], preferred_element_type=F32)
           + jnp.dot(t_ref[...], w_ref[da:, :], preferred_element_type=F32))
    xn = x_ref[...] + g1_ref[...] * mix
    xn_ref[...] = xn
    y = xn * lax.rsqrt(jnp.mean(xn * xn, axis=-1, keepdims=True) + EPS) * ng_ref[...]
    h2_ref[...] = (y * (1.0 + sc_ref[...]) + sh_ref[...]).T.astype(h2_ref.dtype)


def _out_proj(z, ws, bs_t, attn, w_out, x2, g1, ng, sh, sc, tm):
    l, d = x2.shape
    da, dv = GMLP_GROUPS * CHUNK, attn.shape[1]
    cst = lambda i: (0, 0)
    return pl.pallas_call(
        _out_proj_kernel,
        grid=(l // tm,),
        in_specs=[pl.BlockSpec((tm, da), lambda i: (i, 0)),
                  pl.BlockSpec((tm, da), lambda i: (i, 1)),
                  pl.BlockSpec((GMLP_GROUPS, CHUNK, CHUNK), lambda i: (0, 0, 0)),
                  pl.BlockSpec((CHUNK, GMLP_GROUPS), cst),
                  pl.BlockSpec((tm, dv), lambda i: (i, 0)),
                  pl.BlockSpec((da + dv, d), cst),
                  pl.BlockSpec((tm, d), lambda i: (i, 0)),
                  pl.BlockSpec((1, d), cst), pl.BlockSpec((1, d), cst),
                  pl.BlockSpec((1, d), cst), pl.BlockSpec((1, d), cst)],
        out_specs=[pl.BlockSpec((tm, d), lambda i: (i, 0)),
                   pl.BlockSpec((d, tm), lambda i: (0, i))],
        out_shape=[jax.ShapeDtypeStruct((l, d), F32), jax.ShapeDtypeStruct((d, l), MXU_DTYPE)],
        compiler_params=_cparams(("arbitrary",)), name="out_proj",
    )(z, z, ws, bs_t, attn, w_out, x2, g1, ng, sh, sc)


SUBL = 8

_SORT16 = ((0, 1), (2, 3), (0, 2), (1, 3), (1, 2), (4, 5), (6, 7), (4, 6), (5, 7), (5, 6), (0, 4),
           (2, 6), (2, 4), (1, 5), (3, 7), (3, 5), (1, 2), (3, 4), (5, 6), (8, 9), (10, 11), (8, 10),
           (9, 11), (9, 10), (12, 13), (14, 15), (12, 14), (13, 15), (13, 14), (8, 12), (10, 14),
           (10, 12), (9, 13), (11, 15), (11, 13), (9, 10), (11, 12), (13, 14), (0, 8), (4, 12),
           (4, 8), (2, 10), (6, 14), (6, 10), (2, 4), (6, 8), (10, 12), (1, 9), (5, 13), (5, 9),
           (3, 11), (7, 15), (7, 11), (3, 5), (7, 9), (11, 13), (1, 2), (3, 4), (5, 6), (7, 8),
           (9, 10), (11, 12), (13, 14))


def _peer_sel_kernel(ht_ref, wqt_ref, keys_ref, cnt_ref, ea_ref, rk_ref, eb_ref,
                     qp_ref, at_ref, bt_ref, as2_ref, bs2_ref):
    tm = ht_ref.shape[1]
    qp_ref[...] = jnp.dot(wqt_ref[...], ht_ref[...],
                          preferred_element_type=F32).astype(qp_ref.dtype)
    nh = cnt_ref.shape[0]

    def largest16(v, emit):
        n = len(v)
        for i, j in _SORT16:
            if j < n:
                v[i], v[j] = jnp.maximum(v[i], v[j]), jnp.minimum(v[i], v[j])
        for r in range(PEER_TOPK):
            m = jnp.max(v[0], axis=0, keepdims=True)
            emit(r, m)
            hit = v[0] == m
            for k in range(min(n, PEER_TOPK - 1 - r)):
                v[k] = jnp.where(hit, v[k + 1] if k + 1 < n else NEG_BIG, v[k])

    def top16(s, out_ref):
        def emit(r, m):
            out_ref[r:r + 1, :] = m
        largest16([s[k * SUBL:(k + 1) * SUBL, :] for k in range(N_KEYS // SUBL)], emit)

    def strip(h, c0, unit):
        as_ref, bs_ref = as2_ref.at[unit], bs2_ref.at[unit]
        cols = pl.ds(c0, LANES)
        a = at_ref[:, cols]
        b = bt_ref[:, cols]
        top16(a, as_ref)
        top16(b, bs_ref)
        bs_lo = bs_ref[0:SUBL, :]
        cands = [as_ref[0:1, :] + bs_lo, as_ref[0:1, :] + bs_ref[SUBL:PEER_TOPK, :]]
        cands += [as_ref[i:i + 1, :] + bs_lo for i in range(1, SUBL)]
        cands.append(as_ref[SUBL:PEER_TOPK, :] + bs_ref[0:1, :])
        picked = []
        largest16(cands, lambda r, m: picked.append(m))
        cmax, tau = picked[0], picked[-1]
        zsum = jnp.ones_like(cmax)
        for m in picked[1:]:
            zsum = zsum + jnp.exp(m - cmax)
        bs = bs_ref[...]
        cnt = jnp.zeros(a.shape, F32)
        rank = jnp.full(b.shape, float(PEER_TOPK), F32)
        for r in range(PEER_TOPK):
            ar = as_ref[r:r + 1, :]
            n_r = jnp.sum(jnp.where(ar + bs >= tau, 1.0, 0.0), axis=0, keepdims=True)
            cnt = jnp.where(a == ar, n_r, cnt)
            rank = jnp.where(b == bs_ref[r:r + 1, :], float(r), rank)
        cnt_ref[h, :, cols] = cnt
        ea_ref[h, :, cols] = jnp.exp(a - as_ref[0:1, :])
        rk_ref[h, :, cols] = rank.astype(rk_ref.dtype)
        eb_ref[h, :, cols] = (jnp.exp(b - bs_ref[0:1, :]) / zsum).astype(eb_ref.dtype)

    def head(h, carry):
        ra = pl.multiple_of(2 * h * N_KEYS, N_KEYS)
        rb = pl.multiple_of((2 * h + 1) * N_KEYS, N_KEYS)
        at_ref[...] = jnp.dot(keys_ref[2 * h], qp_ref[pl.ds(ra, N_KEYS), :],
                              preferred_element_type=F32)
        bt_ref[...] = jnp.dot(keys_ref[2 * h + 1], qp_ref[pl.ds(rb, N_KEYS), :],
                              preferred_element_type=F32)

        def body(s, carry):
            for unit in range(2):
                strip(h, pl.multiple_of((2 * s + unit) * LANES, LANES), unit)
            return carry

        return lax.fori_loop(0, tm // (2 * LANES), body, carry)

    lax.fori_loop(0, nh, head, 0)


def _peer_sel(h2t, wqt, keys, tm):
    d, l = h2t.shape
    nh = keys.shape[0] // 2
    blk = pl.BlockSpec((nh, N_KEYS, tm), lambda i: (0, 0, i))
    by_a = jax.ShapeDtypeStruct((nh, N_KEYS, l), F32)
    by_b = jax.ShapeDtypeStruct((nh, N_KEYS, l), MXU_DTYPE)
    return pl.pallas_call(
        _peer_sel_kernel,
        grid=(l // tm,),
        in_specs=[pl.BlockSpec((d, tm), lambda i: (0, i)),
                  pl.BlockSpec(wqt.shape, lambda i: (0, 0)),
                  pl.BlockSpec(keys.shape, lambda i: (0, 0, 0))],
        out_specs=[blk, blk, blk, blk],
        out_shape=[by_a, by_a, by_b, by_b],
        scratch_shapes=[pltpu.VMEM((wqt.shape[0], tm), MXU_DTYPE),
                        pltpu.VMEM((N_KEYS, tm), F32), pltpu.VMEM((N_KEYS, tm), F32),
                        pltpu.VMEM((2, PEER_TOPK, LANES), F32),
                        pltpu.VMEM((2, PEER_TOPK, LANES), F32)],
        compiler_params=_cparams(("arbitrary",)), name="peer_sel",
    )(h2t, wqt, keys)


def _peer_dense_kernel(ht_ref, u_ref, vt_ref, cnt_ref, ea_ref, rk_ref, eb_ref, x_ref, g2_ref, o_ref,
                       act_ref, acc_ref):
    e = pl.program_id(1)
    n_tiles = pl.num_programs(1) - 1
    nh = cnt_ref.shape[0]
    ni = u_ref.shape[0] // N_KEYS
    gdt = rk_ref.dtype
    tile_rows = 16

    def row_bcast(row):
        one = jnp.broadcast_to(row, (tile_rows, row.shape[1])).astype(gdt)
        return jnp.concatenate([one] * (N_KEYS // tile_rows), axis=0)

    @pl.when(e == 0)
    def _():
        acc_ref[...] = jnp.zeros_like(acc_ref)
        act_ref[...] = jnp.dot(u_ref[...], ht_ref[...], preferred_element_type=F32)

    @pl.when(e > 0)
    def _():
        act = _gelu(act_ref[...]).astype(gdt)
        act_ref[...] = jnp.dot(u_ref[...], ht_ref[...], preferred_element_type=F32)
        ws = []
        for il in range(ni):
            i_glob = (e - 1) * ni + il
            g = None
            for h in range(nh):
                cnt = row_bcast(cnt_ref[h, pl.ds(i_glob, 1), :])
                ea = row_bcast(ea_ref[h, pl.ds(i_glob, 1), :])
                t = jnp.where(rk_ref[h] < cnt, eb_ref[h], jnp.zeros((), gdt)) * ea
                g = t if g is None else g + t
            ws.append(g * act[il * N_KEYS:(il + 1) * N_KEYS, :])
        w = jnp.concatenate(ws, axis=0)
        acc_ref[...] += jnp.dot(vt_ref[...], w, preferred_element_type=F32)

    @pl.when(e == n_tiles)
    def _():
        o_ref[...] = x_ref[...] + g2_ref[...] * acc_ref[...].T


def _peer_dense(h2t, u, vt, cnt, ea, rk, eb, xn, g2, tt, te):
    d, l = h2t.shape
    n_tiles = u.shape[0] // te
    nh = cnt.shape[0]
    sel = pl.BlockSpec((nh, N_KEYS, tt), lambda i, e: (0, 0, i))
    return pl.pallas_call(
        _peer_dense_kernel,
        grid=(l // tt, n_tiles + 1),
        in_specs=[pl.BlockSpec((d, tt), lambda i, e: (0, i)),
                  pl.BlockSpec((te, d), lambda i, e: (jnp.minimum(e, n_tiles - 1), 0)),
                  pl.BlockSpec((d, te), lambda i, e: (0, jnp.maximum(e - 1, 0))),
                  sel, sel, sel, sel,
                  pl.BlockSpec((tt, d), lambda i, e: (i, 0), pipeline_mode=pl.Buffered(1)),
                  pl.BlockSpec((1, d), lambda i, e: (0, 0))],
        out_specs=pl.BlockSpec((tt, d), lambda i, e: (i, 0)),
        out_shape=jax.ShapeDtypeStruct((l, d), F32),
        scratch_shapes=[pltpu.VMEM((te, tt), F32), pltpu.VMEM((d, tt), F32)],
        compiler_params=_cparams(("arbitrary", "arbitrary")), name="peer_dense",
    )(h2t, u, vt, cnt, ea, rk, eb, xn, g2)


def _rope_tables(n_tok):
    n_rows = n_tok // GRID_W
    rows = jnp.repeat(jnp.arange(n_rows, dtype=F32), GRID_W)
    cols = jnp.tile(jnp.arange(GRID_W, dtype=F32), n_rows)
    n_freq = DIFF_QK // 4
    inv = ROPE_BASE ** (-jnp.arange(n_freq, dtype=F32) / n_freq)
    ang = jnp.concatenate([rows[:, None] * inv, cols[:, None] * inv], axis=-1)
    ang = jnp.repeat(ang, 2, axis=-1)
    sign = jnp.tile(jnp.array([-1.0, 1.0], F32), DIFF_QK // 2)
    cos = jnp.tile(jnp.cos(ang), (1, LANES // DIFF_QK))
    sin = jnp.tile(jnp.sin(ang) * sign, (1, LANES // DIFF_QK))
    return cos, sin


def kernel(x, c, ctx, c_ctx, w_ada, b_ada, norm1_g, norm2_g, w_in, gmlp_ln_g, gmlp_ws, gmlp_bs,
           q_norm_g, k_norm_g, lambda_q1, lambda_k1, lambda_q2, lambda_k2, subln_g, w_out,
           peer_wq, peer_keys, peer_u, peer_v):
    depth = w_ada.shape[0]
    assert depth == 1 and x.shape[0] == 1
    _, seq, d = x.shape
    lc = ctx.shape[1]
    da = GMLP_GROUPS * CHUNK
    dqk = DIFF_HEADS * 2 * DIFF_QK
    lam_init = 0.8 - 0.6 * math.exp(-0.3 * 0)
    row = lambda a: a.reshape(1, -1).astype(F32)

    cond8 = jnp.zeros((8, d), F32).at[0].set(c[0]).at[1].set(c_ctx)
    mod = _ada(cond8, w_ada[0], b_ada[0])
    sh1, sc1, g1, sh2, sc2, g2 = [mod[0:1, k * d:(k + 1) * d] for k in range(6)]
    sh1c, sc1c = mod[1:2, 0:d], mod[1:2, d:2 * d]

    w_in_b = w_in[0].astype(MXU_DTYPE)
    ones_bd = jnp.kron(jnp.eye(LANES // DIFF_QK, dtype=F32),
                       jnp.ones((DIFF_QK, DIFF_QK), F32)).astype(MXU_DTYPE)
    qg = jnp.tile(row(q_norm_g[0]), (1, dqk // DIFF_QK))
    kg = jnp.tile(row(k_norm_g[0]), (1, dqk // DIFF_QK))
    lng = row(gmlp_ln_g[0])
    cos, sin = _rope_tables(seq)
    q_scale = (DIFF_QK ** -0.5) * math.log2(math.e)

    z, vt = _in_proj(x[0], row(norm1_g[0]), sh1, sc1, w_in_b, lng, qg, kg, cos, sin, ones_bd,
                     ("u", "va", "q", "k", "v"), q_scale, tm=min(ROWS_IN, seq))
    zc, vct = _in_proj(ctx[0], row(norm1_g[0]), sh1c, sc1c, w_in_b[:, 2 * da + dqk:], lng, qg, kg,
                       jnp.ones((lc, LANES), F32), jnp.zeros((lc, LANES), F32), ones_bd,
                       ("k", "v"), 1.0, tm=lc)

    attn = _attention(z, vt, zc, vct, subln_g[0], lambda_q1[0], lambda_k1[0], lambda_q2[0],
                      lambda_k2[0], lam_init, tq=min(ATTN_Q, seq), tk=min(ATTN_KEYS, seq))
    xn, h2t = _out_proj(z, gmlp_ws[0].astype(MXU_DTYPE), gmlp_bs[0].T.astype(F32), attn,
                        w_out[0].astype(MXU_DTYPE), x[0], g1, row(norm2_g[0]), sh2, sc2,
                        tm=min(ROWS_OUT, seq))

    keys = peer_keys[0].reshape(PEER_HEADS * 2, N_KEYS, -1).astype(MXU_DTYPE)
    cnt, ea, rk, eb = _peer_sel(h2t, peer_wq[0].T.astype(MXU_DTYPE), keys,
                                tm=min(SEL_TOKENS, seq))
    out = _peer_dense(h2t, peer_u[0].astype(MXU_DTYPE), peer_v[0].T.astype(MXU_DTYPE),
                      cnt, ea, rk, eb, xn, g2, tt=min(DENSE_TOKENS, seq), te=DENSE_EXPERTS)
    return out[None]
```

```python
import functools
import math

import jax
import jax.numpy as jnp
from jax import lax
from jax.experimental import pallas as pl
from jax.experimental.pallas import tpu as pltpu

F32 = jnp.float32
MXU_DTYPE = jnp.bfloat16

EPS = 1e-6
GRID_W = 64
ROPE_BASE = 10000.0

GMLP_GROUPS = 8
CHUNK = 128
DIFF_HEADS = 8
DIFF_QK = 64
DIFF_V = 128
PEER_HEADS = 8
N_KEYS = 128
PEER_TOPK = 16

LANES = 128
NEG_BIG = -1e30

VMEM_LIMIT = 56 * 1024 * 1024

ADA_COLS = 1024
ROWS_IN = 512
ATTN_Q = 256
ATTN_UNITS = 2
ATTN_KEYS = 1024
ROWS_OUT = 512
OUT_UNITS = 2
SEL_TOKENS = 512
DENSE_TOKENS = 512
DENSE_EXPERTS = 1024


def _cparams(sem):
    return pltpu.CompilerParams(dimension_semantics=sem, vmem_limit_bytes=VMEM_LIMIT)


def _gelu(x):
    return 0.5 * x * (1.0 + lax.erf(x * (2.0 ** -0.5)))


def _ada_kernel(c_ref, w_ref, b_ref, o_ref):
    c = c_ref[...]
    s = c * jax.nn.sigmoid(c)
    o_ref[...] = jnp.dot(s.astype(MXU_DTYPE), w_ref[...].astype(MXU_DTYPE),
                         preferred_element_type=F32) + b_ref[...]


def _ada(cond8, w_ada, b_ada):
    d, n = w_ada.shape
    tn = ADA_COLS
    return pl.pallas_call(
        _ada_kernel,
        grid=(n // tn,),
        in_specs=[pl.BlockSpec((8, d), lambda j: (0, 0)),
                  pl.BlockSpec((d, tn), lambda j: (0, j)),
                  pl.BlockSpec((1, tn), lambda j: (0, j))],
        out_specs=pl.BlockSpec((8, tn), lambda j: (0, j)),
        out_shape=jax.ShapeDtypeStruct((8, n), F32),
        compiler_params=_cparams(("arbitrary",)), name="ada",
    )(cond8, w_ada, b_ada.reshape(1, n))


def _group_sum_64(xx, ones_bd):
    outs = []
    for b in range(xx.shape[1] // LANES):
        blk = xx[:, b * LANES:(b + 1) * LANES]
        hi = blk.astype(MXU_DTYPE)
        lo = (blk - hi.astype(F32)).astype(MXU_DTYPE)
        outs.append(jnp.dot(hi, ones_bd, preferred_element_type=F32)
                    + jnp.dot(lo, ones_bd, preferred_element_type=F32))
    return jnp.concatenate(outs, axis=1)


def _swap_pairs(x):
    n = x.shape[1]
    lane = lax.broadcasted_iota(jnp.int32, x.shape, 1)
    nxt = pltpu.roll(x, n - 1, 1)
    prv = pltpu.roll(x, 1, 1)
    return jnp.where((lane & 1) == 0, nxt, prv)


def _in_proj_kernel(sections, q_scale,
                    x_ref, ng_ref, sh_ref, sc_ref, w_ref, lng_ref, qg_ref, kg_ref,
                    cos_ref, sin_ref, ones_ref, o_ref, vt_ref):
    x = x_ref[...]
    y = x * lax.rsqrt(jnp.mean(x * x, axis=-1, keepdims=True) + EPS) * ng_ref[...]
    h = (y * (1.0 + sc_ref[...]) + sh_ref[...]).astype(w_ref.dtype)
    tn = w_ref.shape[1] // len(sections)

    def qk_norm_rope(z, g_ref, scale):
        ms = _group_sum_64(z * z, ones_ref[...]) * (1.0 / DIFF_QK)
        y = z * lax.rsqrt(ms + EPS) * g_ref[...]
        reps = tn // LANES
        cos = jnp.concatenate([cos_ref[...]] * reps, axis=1)
        sin = jnp.concatenate([sin_ref[...]] * reps, axis=1)
        y = y * cos + _swap_pairs(y) * sin
        if scale != 1.0:
            y = y * scale
        return y

    for idx, kind in enumerate(sections):
        acc = jnp.dot(h, w_ref[:, idx * tn:(idx + 1) * tn], preferred_element_type=F32)
        cols = slice(idx * tn, (idx + 1) * tn)
        if kind == "u":
            o_ref[:, cols] = _gelu(acc).astype(o_ref.dtype)
        elif kind == "va":
            g = _gelu(acc)
            for b in range(tn // CHUNK):
                blk = g[:, b * CHUNK:(b + 1) * CHUNK]
                mu = jnp.mean(blk, axis=-1, keepdims=True)
                d = blk - mu
                y = d * lax.rsqrt(jnp.mean(d * d, axis=-1, keepdims=True) + EPS)
                y = y * lng_ref[:, b * CHUNK:(b + 1) * CHUNK]
                o_ref[:, idx * tn + b * CHUNK:idx * tn + (b + 1) * CHUNK] = y.astype(o_ref.dtype)
        elif kind == "q":
            o_ref[:, cols] = qk_norm_rope(acc, qg_ref, q_scale).astype(o_ref.dtype)
        elif kind == "k":
            o_ref[:, cols] = qk_norm_rope(acc, kg_ref, 1.0).astype(o_ref.dtype)
        else:
            vt_ref[...] = acc.T.astype(vt_ref.dtype)


def _in_proj(x2, ng, sh, sc, w, lng, qg, kg, cos, sin, ones_bd, sections, q_scale, tm):
    l, d = x2.shape
    n = w.shape[1]
    tn = n // len(sections)
    assert sections[-1] == "v"
    cst = lambda i: (0, 0)
    return pl.pallas_call(
        functools.partial(_in_proj_kernel, sections, q_scale),
        grid=(l // tm,),
        in_specs=[pl.BlockSpec((tm, d), lambda i: (i, 0)),
                  pl.BlockSpec((1, d), cst), pl.BlockSpec((1, d), cst), pl.BlockSpec((1, d), cst),
                  pl.BlockSpec((d, n), cst, pipeline_mode=pl.Buffered(1)),
                  pl.BlockSpec((1, tn), cst), pl.BlockSpec((1, tn), cst), pl.BlockSpec((1, tn), cst),
                  pl.BlockSpec((tm, LANES), lambda i: (i, 0)),
                  pl.BlockSpec((tm, LANES), lambda i: (i, 0)),
                  pl.BlockSpec((LANES, LANES), cst)],
        out_specs=[pl.BlockSpec((tm, n - tn), lambda i: (i, 0)),
                   pl.BlockSpec((tn, tm), lambda i: (0, i))],
        out_shape=[jax.ShapeDtypeStruct((l, n - tn), MXU_DTYPE),
                   jax.ShapeDtypeStruct((tn, l), MXU_DTYPE)],
        compiler_params=_cparams(("arbitrary",)), name="in_proj",
    )(x2, ng, sh, sc, w, lng, qg, kg, cos, sin, ones_bd)


def _attn_kernel(tk, lam_init,
                 q_ref, k_ref, vt_ref, kc_ref, vct_ref, g_ref, l1_ref, l2_ref, l3_ref, l4_ref,
                 o_ref, sa_ref, sb_ref, pa_ref, pb_ref):
    tq = q_ref.shape[0] // ATTN_UNITS
    for unit in range(ATTN_UNITS):
        rows = slice(unit * tq, (unit + 1) * tq)
        _attn_tile(tk, lam_init, q_ref.at[rows, :], k_ref, vt_ref, kc_ref, vct_ref, g_ref,
                   l1_ref, l2_ref, l3_ref, l4_ref, o_ref.at[rows, :],
                   sa_ref.at[unit], sb_ref.at[unit], pa_ref.at[unit], pb_ref.at[unit])


def _attn_tile(tk, lam_init,
               q_ref, k_ref, vt_ref, kc_ref, vct_ref, g_ref, l1_ref, l2_ref, l3_ref, l4_ref,
               o_ref, sa_ref, sb_ref, pa_ref, pb_ref):
    tq = q_ref.shape[0]
    qt = q_ref[...].astype(F32).T.astype(MXU_DTYPE)
    sub = lax.broadcasted_iota(jnp.int32, qt.shape, 0)
    zero = jnp.zeros_like(qt)
    qbd = jnp.concatenate([jnp.where(sub < DIFF_QK, qt, zero),
                           jnp.where(sub >= DIFF_QK, qt, zero)], axis=1)
    n_chunks = k_ref.shape[0] // tk

    def scores(c, s_ref):
        s_ref[...] = jnp.dot(k_ref[c * tk:(c + 1) * tk, :], qbd, preferred_element_type=F32)

    def softmax(s_ref, p_ref, state):
        m, l, _, acc = state
        s = s_ref[...]
        m_new = jnp.maximum(m, jnp.max(s, axis=0, keepdims=True))
        p = jnp.exp2(s - m_new)
        alpha = jnp.exp2(m - m_new)
        l = alpha * l + jnp.sum(p, axis=0, keepdims=True)
        p_ref[...] = p.astype(p_ref.dtype)
        return m_new, l, alpha, acc

    def values(c, p_ref, state):
        m, l, alpha, acc = state
        vt = vt_ref[:, c * tk:(c + 1) * tk]
        return m, l, alpha, alpha * acc + jnp.dot(vt, p_ref[...], preferred_element_type=F32)

    s = jnp.dot(kc_ref[...], qbd, preferred_element_type=F32)
    m = jnp.max(s, axis=0, keepdims=True)
    p = jnp.exp2(s - m)
    state = (m, jnp.sum(p, axis=0, keepdims=True), jnp.ones_like(m),
             jnp.dot(vct_ref[...], p.astype(MXU_DTYPE), preferred_element_type=F32))

    s_bufs, p_bufs = (sa_ref, sb_ref), (pa_ref, pb_ref)
    scores(0, s_bufs[0])
    for c in range(n_chunks):
        if c + 1 < n_chunks:
            scores(c + 1, s_bufs[(c + 1) % 2])
        state = softmax(s_bufs[c % 2], p_bufs[c % 2], state)
        state = values(c, p_bufs[c % 2], state)
    _, l, _, acc = state

    lam = (jnp.exp(jnp.sum(l1_ref[...] * l2_ref[...], axis=-1, keepdims=True))
           - jnp.exp(jnp.sum(l3_ref[...] * l4_ref[...], axis=-1, keepdims=True)) + lam_init)
    o = acc[:, :tq] / l[:, :tq] - lam * (acc[:, tq:] / l[:, tq:])
    y = o * lax.rsqrt(jnp.mean(o * o, axis=0, keepdims=True) + EPS) * g_ref[...]
    o_ref[...] = (y * (1.0 - lam_init)).T.astype(o_ref.dtype)


def _attention(z, vt, zc, vct, subln_g, lq1, lk1, lq2, lk2, lam_init, tq, tk):
    l = z.shape[0]
    lc = zc.shape[0]
    nh = DIFF_HEADS
    qb, kb = 2 * nh, 3 * nh
    vec = lambda a: a.reshape(1, -1).astype(F32)
    cst = lambda h, i: (0, 0)
    units = ATTN_UNITS if l % (ATTN_UNITS * tq) == 0 else 1
    assert units == ATTN_UNITS
    tq_step = units * tq
    return pl.pallas_call(
        functools.partial(_attn_kernel, tk, lam_init),
        grid=(nh, l // tq_step),
        in_specs=[pl.BlockSpec((tq_step, LANES), lambda h, i: (i, qb + h)),
                  pl.BlockSpec((l, LANES), lambda h, i: (0, kb + h)),
                  pl.BlockSpec((DIFF_V, l), lambda h, i: (h, 0)),
                  pl.BlockSpec((lc, LANES), lambda h, i: (0, h)),
                  pl.BlockSpec((DIFF_V, lc), lambda h, i: (h, 0)),
                  pl.BlockSpec((DIFF_V, 1), cst),
                  pl.BlockSpec((1, DIFF_QK), cst), pl.BlockSpec((1, DIFF_QK), cst),
                  pl.BlockSpec((1, DIFF_QK), cst), pl.BlockSpec((1, DIFF_QK), cst)],
        out_specs=pl.BlockSpec((tq_step, DIFF_V), lambda h, i: (i, h)),
        out_shape=jax.ShapeDtypeStruct((l, nh * DIFF_V), MXU_DTYPE),
        scratch_shapes=[pltpu.VMEM((units, tk, 2 * tq), F32), pltpu.VMEM((units, tk, 2 * tq), F32),
                        pltpu.VMEM((units, tk, 2 * tq), MXU_DTYPE),
                        pltpu.VMEM((units, tk, 2 * tq), MXU_DTYPE)],
        compiler_params=_cparams(("arbitrary", "arbitrary")), name="attention",
    )(z, z, vt, zc, vct, subln_g.reshape(-1, 1).astype(F32), vec(lq1), vec(lk1), vec(lq2), vec(lk2))


def _out_proj_kernel(u_ref, v_ref, ws_ref, bs_ref, t_ref, w_ref, x_ref, g1_ref, ng_ref, sh_ref,
                     sc_ref, xn_ref, h2_ref):
    tm, da = u_ref.shape
    half = tm // OUT_UNITS
    for unit in range(OUT_UNITS):
        r0 = unit * half
        chunks = []
        for c in range(half // CHUNK):
            rows = slice(r0 + c * CHUNK, r0 + (c + 1) * CHUNK)
            groups = []
            for g in range(GMLP_GROUPS):
                cols = slice(g * CHUNK, (g + 1) * CHUNK)
                mixed = jnp.dot(ws_ref[g], v_ref[rows, cols], preferred_element_type=F32)
                mixed = mixed + bs_ref[:, g:g + 1]
                groups.append((u_ref[rows, cols].astype(F32) * mixed).astype(t_ref.dtype))
            chunks.append(jnp.concatenate(groups, axis=1))
        a_out = jnp.concatenate(chunks, axis=0)
        hrows = slice(r0, r0 + half)
        mix = (jnp.dot(a_out, w_ref[0:da, :], preferred_element_type=F32)
               + jnp.dot(t_ref[hrows, :], w_ref[da:, :], preferred_element_type=F32))
        xn = x_ref[hrows, :] + g1_ref[...] * mix
        xn_ref[hrows, :] = xn
        y = xn * lax.rsqrt(jnp.mean(xn * xn, axis=-1, keepdims=True) + EPS) * ng_ref[...]
        h2_ref[:, hrows] = (y * (1.0 + sc_ref[...]) + sh_ref[...]).T.astype(h2_ref.dtype)


def _out_proj(z, ws, bs_t, attn, w_out, x2, g1, ng, sh, sc, tm):
    l, d = x2.shape
    da, dv = GMLP_GROUPS * CHUNK, attn.shape[1]
    cst = lambda i: (0, 0)
    return pl.pallas_call(
        _out_proj_kernel,
        grid=(l // tm,),
        in_specs=[pl.BlockSpec((tm, da), lambda i: (i, 0)),
                  pl.BlockSpec((tm, da), lambda i: (i, 1)),
                  pl.BlockSpec((GMLP_GROUPS, CHUNK, CHUNK), lambda i: (0, 0, 0)),
                  pl.BlockSpec((CHUNK, GMLP_GROUPS), cst),
                  pl.BlockSpec((tm, dv), lambda i: (i, 0)),
                  pl.BlockSpec((da + dv, d), cst),
                  pl.BlockSpec((tm, d), lambda i: (i, 0)),
                  pl.BlockSpec((1, d), cst), pl.BlockSpec((1, d), cst),
                  pl.BlockSpec((1, d), cst), pl.BlockSpec((1, d), cst)],
        out_specs=[pl.BlockSpec((tm, d), lambda i: (i, 0)),
                   pl.BlockSpec((d, tm), lambda i: (0, i))],
        out_shape=[jax.ShapeDtypeStruct((l, d), F32), jax.ShapeDtypeStruct((d, l), MXU_DTYPE)],
        compiler_params=_cparams(("arbitrary",)), name="out_proj",
    )(z, z, ws, bs_t, attn, w_out, x2, g1, ng, sh, sc)


SUBL = 8

_SORT16 = ((0, 1), (2, 3), (0, 2), (1, 3), (1, 2), (4, 5), (6, 7), (4, 6), (5, 7), (5, 6), (0, 4),
           (2, 6), (2, 4), (1, 5), (3, 7), (3, 5), (1, 2), (3, 4), (5, 6), (8, 9), (10, 11), (8, 10),
           (9, 11), (9, 10), (12, 13), (14, 15), (12, 14), (13, 15), (13, 14), (8, 12), (10, 14),
           (10, 12), (9, 13), (11, 15), (11, 13), (9, 10), (11, 12), (13, 14), (0, 8), (4, 12),
           (4, 8), (2, 10), (6, 14), (6, 10), (2, 4), (6, 8), (10, 12), (1, 9), (5, 13), (5, 9),
           (3, 11), (7, 15), (7, 11), (3, 5), (7, 9), (11, 13), (1, 2), (3, 4), (5, 6), (7, 8),
           (9, 10), (11, 12), (13, 14))


def _peer_sel_kernel(ht_ref, wqt_ref, keys_ref, cnt_ref, ea_ref, rk_ref, eb_ref,
                     qp_ref, at_ref, bt_ref, as2_ref, bs2_ref):
    tm = ht_ref.shape[1]
    qp_ref[...] = jnp.dot(wqt_ref[...], ht_ref[...],
                          preferred_element_type=F32).astype(qp_ref.dtype)
    nh = cnt_ref.shape[0]

    def largest16(v, emit):
        n = len(v)
        for i, j in _SORT16:
            if j < n:
                v[i], v[j] = jnp.maximum(v[i], v[j]), jnp.minimum(v[i], v[j])
        for r in range(PEER_TOPK):
            m = jnp.max(v[0], axis=0, keepdims=True)
            emit(r, m)
            hit = v[0] == m
            for k in range(min(n, PEER_TOPK - 1 - r)):
                v[k] = jnp.where(hit, v[k + 1] if k + 1 < n else NEG_BIG, v[k])

    def top16(s, out_ref):
        def emit(r, m):
            out_ref[r:r + 1, :] = m
        largest16([s[k * SUBL:(k + 1) * SUBL, :] for k in range(N_KEYS // SUBL)], emit)

    def strip(h, c0, unit):
        as_ref, bs_ref = as2_ref.at[unit], bs2_ref.at[unit]
        cols = pl.ds(c0, LANES)
        a = at_ref[:, cols]
        b = bt_ref[:, cols]
        top16(a, as_ref)
        top16(b, bs_ref)
        bs_lo = bs_ref[0:SUBL, :]
        cands = [as_ref[0:1, :] + bs_lo, as_ref[0:1, :] + bs_ref[SUBL:PEER_TOPK, :]]
        cands += [as_ref[i:i + 1, :] + bs_lo for i in range(1, SUBL)]
        cands.append(as_ref[SUBL:PEER_TOPK, :] + bs_ref[0:1, :])
        picked = []
        largest16(cands, lambda r, m: picked.append(m))
        cmax, tau = picked[0], picked[-1]
        zsum = jnp.ones_like(cmax)
        for m in picked[1:]:
            zsum = zsum + jnp.exp(m - cmax)
        bs = bs_ref[...]
        cnt = jnp.zeros(a.shape, F32)
        rank = jnp.full(b.shape, float(PEER_TOPK), F32)
        for r in range(PEER_TOPK):
            ar = as_ref[r:r + 1, :]
            n_r = jnp.sum(jnp.where(ar + bs >= tau, 1.0, 0.0), axis=0, keepdims=True)
            cnt = jnp.where(a == ar, n_r, cnt)
            rank = jnp.where(b == bs_ref[r:r + 1, :], float(r), rank)
        cnt_ref[h, :, cols] = cnt
        ea_ref[h, :, cols] = jnp.exp(a - as_ref[0:1, :])
        rk_ref[h, :, cols] = rank.astype(rk_ref.dtype)
        eb_ref[h, :, cols] = (jnp.exp(b - bs_ref[0:1, :]) / zsum).astype(eb_ref.dtype)

    def head(h, carry):
        ra = pl.multiple_of(2 * h * N_KEYS, N_KEYS)
        rb = pl.multiple_of((2 * h + 1) * N_KEYS, N_KEYS)
        at_ref[...] = jnp.dot(keys_ref[2 * h], qp_ref[pl.ds(ra, N_KEYS), :],
                              preferred_element_type=F32)
        bt_ref[...] = jnp.dot(keys_ref[2 * h + 1], qp_ref[pl.ds(rb, N_KEYS), :],
                              preferred_element_type=F32)

        def body(s, carry):
            for unit in range(2):
                strip(h, pl.multiple_of((2 * s + unit) * LANES, LANES), unit)
            return carry

        return lax.fori_loop(0, tm // (2 * LANES), body, carry)

    lax.fori_loop(0, nh, head, 0)


def _peer_sel(h2t, wqt, keys, tm):
    d, l = h2t.shape
    nh = keys.shape[0] // 2
    blk = pl.BlockSpec((nh, N_KEYS, tm), lambda i: (0, 0, i))
    by_a = jax.ShapeDtypeStruct((nh, N_KEYS, l), F32)
    by_b = jax.ShapeDtypeStruct((nh, N_KEYS, l), MXU_DTYPE)
    return pl.pallas_call(
        _peer_sel_kernel,
        grid=(l // tm,),
        in_specs=[pl.BlockSpec((d, tm), lambda i: (0, i)),
                  pl.BlockSpec(wqt.shape, lambda i: (0, 0)),
                  pl.BlockSpec(keys.shape, lambda i: (0, 0, 0))],
        out_specs=[blk, blk, blk, blk],
        out_shape=[by_a, by_a, by_b, by_b],
        scratch_shapes=[pltpu.VMEM((wqt.shape[0], tm), MXU_DTYPE),
                        pltpu.VMEM((N_KEYS, tm), F32), pltpu.VMEM((N_KEYS, tm), F32),
                        pltpu.VMEM((2, PEER_TOPK, LANES), F32),
                        pltpu.VMEM((2, PEER_TOPK, LANES), F32)],
        compiler_params=_cparams(("arbitrary",)), name="peer_sel",
    )(h2t, wqt, keys)


def _peer_dense_kernel(ht_ref, u_ref, vt_ref, cnt_ref, ea_ref, rk_ref, eb_ref, x_ref, g2_ref, o_ref,
                       act_ref, acc_ref):
    e = pl.program_id(1)
    n_tiles = pl.num_programs(1) - 1
    nh = cnt_ref.shape[0]
    ni = u_ref.shape[0] // N_KEYS
    gdt = rk_ref.dtype
    tile_rows = 16

    def row_bcast(row):
        one = jnp.broadcast_to(row, (tile_rows, row.shape[1])).astype(gdt)
        return jnp.concatenate([one] * (N_KEYS // tile_rows), axis=0)

    @pl.when(e == 0)
    def _():
        acc_ref[...] = jnp.zeros_like(acc_ref)
        act_ref[...] = jnp.dot(u_ref[...], ht_ref[...], preferred_element_type=F32)

    @pl.when(e > 0)
    def _():
        act = _gelu(act_ref[...]).astype(gdt)
        act_ref[...] = jnp.dot(u_ref[...], ht_ref[...], preferred_element_type=F32)
        ws = []
        for il in range(ni):
            i_glob = (e - 1) * ni + il
            g = None
            for h in range(nh):
                cnt = row_bcast(cnt_ref[h, pl.ds(i_glob, 1), :])
                ea = row_bcast(ea_ref[h, pl.ds(i_glob, 1), :])
                t = jnp.where(rk_ref[h] < cnt, eb_ref[h], jnp.zeros((), gdt)) * ea
                g = t if g is None else g + t
            ws.append(g * act[il * N_KEYS:(il + 1) * N_KEYS, :])
        w = jnp.concatenate(ws, axis=0)
        acc_ref[...] += jnp.dot(vt_ref[...], w, preferred_element_type=F32)

    @pl.when(e == n_tiles)
    def _():
        o_ref[...] = x_ref[...] + g2_ref[...] * acc_ref[...].T


def _peer_dense(h2t, u, vt, cnt, ea, rk, eb, xn, g2, tt, te):
    d, l = h2t.shape
    n_tiles = u.shape[0] // te
    nh = cnt.shape[0]
    sel = pl.BlockSpec((nh, N_KEYS, tt), lambda i, e: (0, 0, i))
    return pl.pallas_call(
        _peer_dense_kernel,
        grid=(l // tt, n_tiles + 1),
        in_specs=[pl.BlockSpec((d, tt), lambda i, e: (0, i)),
                  pl.BlockSpec((te, d), lambda i, e: (jnp.minimum(e, n_tiles - 1), 0)),
                  pl.BlockSpec((d, te), lambda i, e: (0, jnp.maximum(e - 1, 0))),
                  sel, sel, sel, sel,
                  pl.BlockSpec((tt, d), lambda i, e: (i, 0), pipeline_mode=pl.Buffered(1)),
                  pl.BlockSpec((1, d), lambda i, e: (0, 0))],
        out_specs=pl.BlockSpec((tt, d), lambda i, e: (i, 0)),
        out_shape=jax.ShapeDtypeStruct((l, d), F32),
        scratch_shapes=[pltpu.VMEM((te, tt), F32), pltpu.VMEM((d, tt), F32)],
        compiler_params=_cparams(("arbitrary", "arbitrary")), name="peer_dense",
    )(h2t, u, vt, cnt, ea, rk, eb, xn, g2)


def _rope_tables(n_tok):
    n_rows = n_tok // GRID_W
    rows = jnp.repeat(jnp.arange(n_rows, dtype=F32), GRID_W)
    cols = jnp.tile(jnp.arange(GRID_W, dtype=F32), n_rows)
    n_freq = DIFF_QK // 4
    inv = ROPE_BASE ** (-jnp.arange(n_freq, dtype=F32) / n_freq)
    ang = jnp.concatenate([rows[:, None] * inv, cols[:, None] * inv], axis=-1)
    ang = jnp.repeat(ang, 2, axis=-1)
    sign = jnp.tile(jnp.array([-1.0, 1.0], F32), DIFF_QK // 2)
    cos = jnp.tile(jnp.cos(ang), (1, LANES // DIFF_QK))
    sin = jnp.tile(jnp.sin(ang) * sign, (1, LANES // DIFF_QK))
    return cos, sin


def kernel(x, c, ctx, c_ctx, w_ada, b_ada, norm1_g, norm2_g, w_in, gmlp_ln_g, gmlp_ws, gmlp_bs,
           q_norm_g, k_norm_g, lambda_q1, lambda_k1, lambda_q2, lambda_k2, subln_g, w_out,
           peer_wq, peer_keys, peer_u, peer_v):
    depth = w_ada.shape[0]
    assert depth == 1 and x.shape[0] == 1
    _, seq, d = x.shape
    lc = ctx.shape[1]
    da = GMLP_GROUPS * CHUNK
    dqk = DIFF_HEADS * 2 * DIFF_QK
    lam_init = 0.8 - 0.6 * math.exp(-0.3 * 0)
    row = lambda a: a.reshape(1, -1).astype(F32)

    cond8 = jnp.zeros((8, d), F32).at[0].set(c[0]).at[1].set(c_ctx)
    mod = _ada(cond8, w_ada[0], b_ada[0])
    sh1, sc1, g1, sh2, sc2, g2 = [mod[0:1, k * d:(k + 1) * d] for k in range(6)]
    sh1c, sc1c = mod[1:2, 0:d], mod[1:2, d:2 * d]

    w_in_b = w_in[0].astype(MXU_DTYPE)
    ones_bd = jnp.kron(jnp.eye(LANES // DIFF_QK, dtype=F32),
                       jnp.ones((DIFF_QK, DIFF_QK), F32)).astype(MXU_DTYPE)
    qg = jnp.tile(row(q_norm_g[0]), (1, dqk // DIFF_QK))
    kg = jnp.tile(row(k_norm_g[0]), (1, dqk // DIFF_QK))
    lng = row(gmlp_ln_g[0])
    cos, sin = _rope_tables(seq)
    q_scale = (DIFF_QK ** -0.5) * math.log2(math.e)

    z, vt = _in_proj(x[0], row(norm1_g[0]), sh1, sc1, w_in_b, lng, qg, kg, cos, sin, ones_bd,
                     ("u", "va", "q", "k", "v"), q_scale, tm=min(ROWS_IN, seq))
    zc, vct = _in_proj(ctx[0], row(norm1_g[0]), sh1c, sc1c, w_in_b[:, 2 * da + dqk:], lng, qg, kg,
                       jnp.ones((lc, LANES), F32), jnp.zeros((lc, LANES), F32), ones_bd,
                       ("k", "v"), 1.0, tm=lc)

    attn = _attention(z, vt, zc, vct, subln_g[0], lambda_q1[0], lambda_k1[0], lambda_q2[0],
                      lambda_k2[0], lam_init, tq=min(ATTN_Q, seq), tk=min(ATTN_KEYS, seq))
    xn, h2t = _out_proj(z, gmlp_ws[0].astype(MXU_DTYPE), gmlp_bs[0].T.astype(F32), attn,
                        w_out[0].astype(MXU_DTYPE), x[0], g1, row(norm2_g[0]), sh2, sc2,
                        tm=min(ROWS_OUT, seq))

    keys = peer_keys[0].reshape(PEER_HEADS * 2, N_KEYS, -1).astype(MXU_DTYPE)
    cnt, ea, rk, eb = _peer_sel(h2t, peer_wq[0].T.astype(MXU_DTYPE), keys,
                                tm=min(SEL_TOKENS, seq))
    out = _peer_dense(h2t, peer_u[0].astype(MXU_DTYPE), peer_v[0].T.astype(MXU_DTYPE),
                      cnt, ea, rk, eb, xn, g2, tt=min(DENSE_TOKENS, seq), te=DENSE_EXPERTS)
    return out[None]
```
